```python
import math
import jax, jax.numpy as jnp
from jax import lax
import numpy as np

D_MODEL = 1024
BATCH = 2
SEQ = 8192
DEPTH = 1

SSM_GROUP = 16
N_SSM_GROUPS = 32
SSM_WIDTH = SSM_GROUP * N_SSM_GROUPS
SSM_STATE = 64
DT_MIN = 0.001
DT_MAX = 0.1
N_HEADS = 8
HEAD_DIM = 64
ATTN_WIDTH = N_HEADS * HEAD_DIM
MOBA_BLOCK = 256
MOBA_TOPK = 3
Q_CHUNK = 128
ROPE_THETA = 10000.0
D_FF = 2816
RMS_EPS = 1e-6
NEG_INF = -1e30
IN_WIDTH = SSM_WIDTH + 3 * ATTN_WIDTH + 2 * D_MODEL

kernel_name = "hybrid_s5_moba_macaron_block"


def rmsnorm(x, gain):
    xf = x.astype(jnp.float32)
    inv = lax.rsqrt(jnp.mean(xf * xf, axis=-1, keepdims=True) + RMS_EPS)
    return (xf * inv).astype(x.dtype) * gain


def swiglu(h, w_gate, w_up, w_down):
    return (jax.nn.silu(h @ w_gate) * (h @ w_up)) @ w_down


def rotary(x):
    L = x.shape[1]
    pos = jnp.arange(L, dtype=jnp.float32)
    inv_freq = ROPE_THETA ** (-jnp.arange(0, HEAD_DIM, 2, dtype=jnp.float32) / HEAD_DIM)
    ang = pos[:, None] * inv_freq[None, :]
    cos = jnp.cos(ang)[None, :, None, :].astype(x.dtype)
    sin = jnp.sin(ang)[None, :, None, :].astype(x.dtype)
    x1, x2 = jnp.split(x, 2, axis=-1)
    return jnp.concatenate([x1 * cos - x2 * sin, x2 * cos + x1 * sin], axis=-1)


def _complex_linear_combine(e1, e2):
    a1r, a1i, b1r, b1i = e1
    a2r, a2i, b2r, b2i = e2
    ar = a2r * a1r - a2i * a1i
    ai = a2r * a1i + a2i * a1r
    br = a2r * b1r - a2i * b1i + b2r
    bi = a2r * b1i + a2i * b1r + b2i
    return (ar, ai, br, bi)


def s5_mixer(u, a_re, a_im, b_re, b_im, c_re, c_im, d, log_dt):
    Bsz, L, _ = u.shape
    ug = u.astype(jnp.float32).reshape(Bsz, L, N_SSM_GROUPS, SSM_GROUP)
    dt = jnp.exp(log_dt.astype(jnp.float32))[:, None]
    ar = a_re.astype(jnp.float32)
    ai = a_im.astype(jnp.float32)
    mag = jnp.exp(ar * dt)
    abar_r = mag * jnp.cos(ai * dt)
    abar_i = mag * jnp.sin(ai * dt)
    den = ar * ar + ai * ai
    nr = abar_r - 1.0
    ni = abar_i
    fr = ((nr * ar + ni * ai) / den)[..., None]
    fi = ((ni * ar - nr * ai) / den)[..., None]
    br = b_re.astype(jnp.float32)
    bi = b_im.astype(jnp.float32)
    bbar_r = fr * br - fi * bi
    bbar_i = fr * bi + fi * br
    bu_r = jnp.einsum('blgh,gph->blgp', ug, bbar_r)
    bu_i = jnp.einsum('blgh,gph->blgp', ug, bbar_i)
    a_r = jnp.broadcast_to(abar_r, bu_r.shape)
    a_i = jnp.broadcast_to(abar_i, bu_i.shape)
    _, _, xr, xi = lax.associative_scan(_complex_linear_combine, (a_r, a_i, bu_r, bu_i), axis=1)
    y = (jnp.einsum('blgp,ghp->blgh', xr, c_re.astype(jnp.float32))
         - jnp.einsum('blgp,ghp->blgh', xi, c_im.astype(jnp.float32))
         + d.astype(jnp.float32).reshape(N_SSM_GROUPS, SSM_GROUP) * ug)
    return y.reshape(Bsz, L, SSM_WIDTH).astype(u.dtype)


def moba_attention(q, k, v):
    Bsz, H, L, Dh = q.shape
    nb = -(-L // MOBA_BLOCK)
    pad = nb * MOBA_BLOCK - L
    kp = jnp.pad(k, ((0, 0), (0, 0), (0, pad), (0, 0)))
    vp = jnp.pad(v, ((0, 0), (0, 0), (0, pad), (0, 0)))
    kb = kp.reshape(Bsz, H, nb, MOBA_BLOCK, Dh)
    vb = vp.reshape(Bsz, H, nb, MOBA_BLOCK, Dh)
    kmean = jnp.mean(kb.astype(jnp.float32), axis=3)
    ksel = min(MOBA_TOPK, nb)
    scale = HEAD_DIM ** -0.5
    bi = jnp.arange(Bsz)[:, None, None, None]
    hi = jnp.arange(H)[None, :, None, None]
    blk_ids = jnp.arange(nb)

    def chunk(c):
        q0 = c * Q_CHUNK
        qc = lax.dynamic_slice_in_dim(q, q0, Q_CHUNK, axis=2)
        cur = q0 // MOBA_BLOCK
        gate = jnp.einsum('bhqd,bhnd->bhqn', qc.astype(jnp.float32), kmean)
        gate = jnp.where(blk_ids < cur, gate, NEG_INF)
        _, idx = lax.top_k(gate, ksel)
        valid = idx < cur
        kg = kb[bi, hi, idx]
        vg = vb[bi, hi, idx]
        s_sel = jnp.einsum('bhqd,bhqskd->bhqsk', qc, kg).astype(jnp.float32) * scale
        s_sel = jnp.where(valid[..., None], s_sel, NEG_INF).reshape(Bsz, H, Q_CHUNK, ksel * MOBA_BLOCK)
        k_own = lax.dynamic_slice_in_dim(kp, cur * MOBA_BLOCK, MOBA_BLOCK, axis=2)
        v_own = lax.dynamic_slice_in_dim(vp, cur * MOBA_BLOCK, MOBA_BLOCK, axis=2)
        s_own = jnp.einsum('bhqd,bhkd->bhqk', qc, k_own).astype(jnp.float32) * scale
        qpos = q0 + jnp.arange(Q_CHUNK)
        kpos = cur * MOBA_BLOCK + jnp.arange(MOBA_BLOCK)
        s_own = jnp.where(kpos[None, :] <= qpos[:, None], s_own, NEG_INF)
        p = jax.nn.softmax(jnp.concatenate([s_sel, s_own], axis=-1), axis=-1)
        p_sel = p[..., :ksel * MOBA_BLOCK].reshape(Bsz, H, Q_CHUNK, ksel, MOBA_BLOCK).astype(v.dtype)
        p_own = p[..., ksel * MOBA_BLOCK:].astype(v.dtype)
        return (jnp.einsum('bhqsk,bhqskd->bhqd', p_sel, vg)
                + jnp.einsum('bhqk,bhkd->bhqd', p_own, v_own))

    outs = lax.map(chunk, jnp.arange(L // Q_CHUNK))
    return outs.transpose(1, 0, 3, 2, 4).reshape(Bsz, L, H * Dh)


def hybrid_mixer(h, w_in, a_re, a_im, b_re, b_im, c_re, c_im, d, log_dt,
                 glu_w, glu_b, w_branch_ssm, w_branch_attn, w_out):
    Bsz, L, _ = h.shape
    proj = h @ w_in
    offs = np.cumsum([SSM_WIDTH, ATTN_WIDTH, ATTN_WIDTH, ATTN_WIDTH, D_MODEL]).tolist()
    u, q, k, v, g_ssm, g_attn = jnp.split(proj, offs, axis=-1)
    y_ssm = jax.nn.gelu(s5_mixer(u, a_re, a_im, b_re, b_im, c_re, c_im, d, log_dt))
    y_ssm = y_ssm * jax.nn.sigmoid(y_ssm @ glu_w + glu_b)
    branch_a = y_ssm @ w_branch_ssm
    q = rotary(q.reshape(Bsz, L, N_HEADS, HEAD_DIM)).transpose(0, 2, 1, 3)
    k = rotary(k.reshape(Bsz, L, N_HEADS, HEAD_DIM)).transpose(0, 2, 1, 3)
    v = v.reshape(Bsz, L, N_HEADS, HEAD_DIM).transpose(0, 2, 1, 3)
    branch_b = moba_attention(q, k, v) @ w_branch_attn
    merged = jax.nn.sigmoid(g_ssm) * branch_a + jax.nn.sigmoid(g_attn) * branch_b
    return merged @ w_out


def setup_inputs(seed: int = 0) -> dict:
    key = jax.random.key(seed)
    ks = iter(jax.random.split(key, 40))

    def nrm(shape, scale):
        return jax.random.normal(next(ks), shape, jnp.float32) * scale

    def gain(shape):
        return 1.0 + nrm(shape, 0.01)

    G, P, Hg = N_SSM_GROUPS, SSM_STATE, SSM_GROUP
    a_im_base = jnp.pi * jnp.arange(P, dtype=jnp.float32)
    return {
        "x": nrm((BATCH, SEQ, D_MODEL), 1.0),
        "ffn1_norm": gain((DEPTH, D_MODEL)),
        "ffn1_w_gate": nrm((DEPTH, D_MODEL, D_FF), D_MODEL ** -0.5),
        "ffn1_w_up": nrm((DEPTH, D_MODEL, D_FF), D_MODEL ** -0.5),
        "ffn1_w_down": nrm((DEPTH, D_FF, D_MODEL), D_FF ** -0.5),
        "mix_norm": gain((DEPTH, D_MODEL)),
        "w_in": nrm((DEPTH, D_MODEL, IN_WIDTH), D_MODEL ** -0.5),
        "ssm_a_re": -0.5 + nrm((DEPTH, G, P), 0.01),
        "ssm_a_im": a_im_base + nrm((DEPTH, G, P), 0.01),
        "ssm_b_re": nrm((DEPTH, G, P, Hg), (2.0 * Hg) ** -0.5),
        "ssm_b_im": nrm((DEPTH, G, P, Hg), (2.0 * Hg) ** -0.5),
        "ssm_c_re": nrm((DEPTH, G, Hg, P), (2.0 * P) ** -0.5),
        "ssm_c_im": nrm((DEPTH, G, Hg, P), (2.0 * P) ** -0.5),
        "ssm_d": nrm((DEPTH, SSM_WIDTH), 1.0),
        "ssm_log_dt": jax.random.uniform(next(ks), (DEPTH, G), jnp.float32, math.log(DT_MIN), math.log(DT_MAX)),
        "glu_w": nrm((DEPTH, SSM_WIDTH, SSM_WIDTH), SSM_WIDTH ** -0.5),
        "glu_b": nrm((DEPTH, SSM_WIDTH), 0.01),
        "w_branch_ssm": nrm((DEPTH, SSM_WIDTH, D_MODEL), SSM_WIDTH ** -0.5),
        "w_branch_attn": nrm((DEPTH, ATTN_WIDTH, D_MODEL), ATTN_WIDTH ** -0.5),
        "w_out": nrm((DEPTH, D_MODEL, D_MODEL), D_MODEL ** -0.5),
        "ffn2_norm": gain((DEPTH, D_MODEL)),
        "ffn2_w_gate": nrm((DEPTH, D_MODEL, D_FF), D_MODEL ** -0.5),
        "ffn2_w_up": nrm((DEPTH, D_MODEL, D_FF), D_MODEL ** -0.5),
        "ffn2_w_down": nrm((DEPTH, D_FF, D_MODEL), D_FF ** -0.5),
        "final_norm": gain((D_MODEL,)),
    }


def reference(x, ffn1_norm, ffn1_w_gate, ffn1_w_up, ffn1_w_down, mix_norm, w_in,
              ssm_a_re, ssm_a_im, ssm_b_re, ssm_b_im, ssm_c_re, ssm_c_im, ssm_d, ssm_log_dt,
              glu_w, glu_b, w_branch_ssm, w_branch_attn, w_out,
              ffn2_norm, ffn2_w_gate, ffn2_w_up, ffn2_w_down, final_norm):
    for l in range(DEPTH):
        h = rmsnorm(x, ffn1_norm[l])
        x = x + 0.5 * swiglu(h, ffn1_w_gate[l], ffn1_w_up[l], ffn1_w_down[l])
        h = rmsnorm(x, mix_norm[l])
        x = x + hybrid_mixer(h, w_in[l], ssm_a_re[l], ssm_a_im[l], ssm_b_re[l], ssm_b_im[l],
                             ssm_c_re[l], ssm_c_im[l], ssm_d[l], ssm_log_dt[l],
                             glu_w[l], glu_b[l], w_branch_ssm[l], w_branch_attn[l], w_out[l])
        h = rmsnorm(x, ffn2_norm[l])
        x = x + 0.5 * swiglu(h, ffn2_w_gate[l], ffn2_w_up[l], ffn2_w_down[l])
    return rmsnorm(x, final_norm)
```

```python
import functools
import math

import jax
import jax.numpy as jnp
from jax import lax
from jax.experimental import pallas as pl
from jax.experimental.pallas import tpu as pltpu

F32 = jnp.float32
BF16 = jnp.bfloat16

N_HEADS = 8
HEAD_DIM = 64
MOBA_BLOCK = 256
MOBA_TOPK = 3
SSM_GROUP = 16
SSM_STATE = 64
ROPE_THETA = 10000.0
RMS_EPS = 1e-6
NEG_INF = -1e30

LANES = 128
SUBLANES = 8
VMEM_LIMIT_BYTES = 56 * 1024 * 1024

CHUNK = 8
OCTET = LANES // SSM_GROUP
OCT_STATE = OCTET * SSM_STATE


def _rmsnorm(x, gain):
    inv = lax.rsqrt(jnp.mean(x * x, axis=-1, keepdims=True) + RMS_EPS)
    return (x * inv) * gain


def _dot(a, b):
    return jnp.dot(a, b, preferred_element_type=F32)


def _const_spec(shape):
    nd = len(shape)
    return pl.BlockSpec(shape, lambda *_: (0,) * nd, pipeline_mode=pl.Buffered(1))


def _rotary_tile(x, cos, sin_signed, first_half):
    swapped = jnp.where(first_half, pltpu.roll(x, LANES - HEAD_DIM // 2, 1), pltpu.roll(x, HEAD_DIM // 2, 1))
    return x * cos + swapped * sin_signed


def _ffn_inproj_kernel(x_ref, g1_ref, wg_ref, wu_ref, wd_ref, g2_ref, win_ref, wvt_ref, cos_ref, sin_ref,
                       x1_ref, u_ref, q_ref, k_ref, vt_ref, kmean_ref, *, attn_w, ssm_w):
    x = x_ref[...]
    h = _rmsnorm(x, g1_ref[...]).astype(BF16)
    act = (jax.nn.silu(_dot(h, wg_ref[...])) * _dot(h, wu_ref[...])).astype(BF16)
    x1 = x + 0.5 * _dot(act, wd_ref[...])
    x1_ref[...] = x1

    h2 = _rmsnorm(x1, g2_ref[...]).astype(BF16)
    proj = _dot(h2, win_ref[...])
    u_ref[...] = proj[:, :ssm_w]

    cos = cos_ref[...]
    sin_signed = sin_ref[...]
    lane = lax.broadcasted_iota(jnp.int32, cos.shape, 1)
    first_half = (lane % HEAD_DIM) < (HEAD_DIM // 2)
    scale = HEAD_DIM ** -0.5
    tm = x.shape[0]
    nblk = tm // MOBA_BLOCK
    for t in range(attn_w // LANES):
        qs = proj[:, ssm_w + t * LANES: ssm_w + (t + 1) * LANES]
        ks = proj[:, ssm_w + attn_w + t * LANES: ssm_w + attn_w + (t + 1) * LANES]
        q_ref[:, t * LANES:(t + 1) * LANES] = (_rotary_tile(qs, cos, sin_signed, first_half) * scale).astype(BF16)
        kr = _rotary_tile(ks, cos, sin_signed, first_half)
        k_ref[:, t * LANES:(t + 1) * LANES] = kr.astype(BF16)
        for j in range(nblk):
            mean = jnp.mean(kr[j * MOBA_BLOCK:(j + 1) * MOBA_BLOCK], axis=0, keepdims=True)
            kmean_ref[j * SUBLANES:(j + 1) * SUBLANES, t * LANES:(t + 1) * LANES] = jnp.broadcast_to(
                mean, (SUBLANES, LANES))

    vt = lax.dot_general(wvt_ref[...], h2, (((1,), (1,)), ((), ())), preferred_element_type=F32).astype(BF16)
    for j in range(nblk):
        vt_ref[j] = vt[:, j * MOBA_BLOCK:(j + 1) * MOBA_BLOCK]


def _ffn_inproj(x2d, g1, wg, wu, wd, g2, win_uqk, wv_t, cos_t, sin_t, *, seq, tm):
    T, D = x2d.shape
    F = wg.shape[1]
    attn_w = wv_t.shape[0]
    ssm_w = win_uqk.shape[1] - 2 * attn_w
    nblk = tm // MOBA_BLOCK
    tiles_per_seq = seq // tm
    row = lambda i: (i, 0)
    return pl.pallas_call(
        functools.partial(_ffn_inproj_kernel, attn_w=attn_w, ssm_w=ssm_w),
        grid=(T // tm,),
        in_specs=[
            pl.BlockSpec((tm, D), row),
            _const_spec((1, D)), _const_spec((D, F)), _const_spec((D, F)), _const_spec((F, D)),
            _const_spec((1, D)), _const_spec((D, ssm_w + 2 * attn_w)), _const_spec((attn_w, D)),
            pl.BlockSpec((tm, LANES), lambda i: (i % tiles_per_seq, 0)),
            pl.BlockSpec((tm, LANES), lambda i: (i % tiles_per_seq, 0)),
        ],
        out_specs=[
            pl.BlockSpec((tm, D), row),
            pl.BlockSpec((tm, ssm_w), row),
            pl.BlockSpec((tm, attn_w), row),
            pl.BlockSpec((tm, attn_w), row),
            pl.BlockSpec((nblk, attn_w, MOBA_BLOCK), lambda i: (i, 0, 0)),
            pl.BlockSpec((nblk * SUBLANES, attn_w), row),
        ],
        out_shape=[
            jax.ShapeDtypeStruct((T, D), F32),
            jax.ShapeDtypeStruct((T, ssm_w), F32),
            jax.ShapeDtypeStruct((T, attn_w), BF16),
            jax.ShapeDtypeStruct((T, attn_w), BF16),
            jax.ShapeDtypeStruct((T // MOBA_BLOCK, attn_w, MOBA_BLOCK), BF16),
            jax.ShapeDtypeStruct((T // MOBA_BLOCK * SUBLANES, attn_w), F32),
        ],
        compiler_params=pltpu.CompilerParams(
            dimension_semantics=("arbitrary",), vmem_limit_bytes=VMEM_LIMIT_BYTES),
        name="ffn1_inproj",
    )(x2d, g1, wg, wu, wd, g2, win_uqk, wv_t, cos_t, sin_t)


def _s5_kernel(u_ref, w1_ref, wc_ref, a8_ref, pw_ref, d_ref, y_ref,
               lhs_scr, r_scr, e_scr, ein_scr, carry_scr, *, tm, nseg):
    S = OCT_STATE
    CW = CHUNK * LANES
    seg_stride = tm // SUBLANES

    @pl.when(pl.program_id(2) == 0)
    def _():
        carry_scr[...] = jnp.zeros_like(carry_scr)

    for j in range(nseg):
        for i in range(CHUNK):
            lhs_scr[j * SUBLANES:(j + 1) * SUBLANES, i * LANES:(i + 1) * LANES] = (
                u_ref[pl.ds(CHUNK * j + i, SUBLANES, stride=seg_stride), :])

    r_scr[...] = _dot(lhs_scr[...].astype(BF16), w1_ref[0])

    a8 = a8_ref[0]
    ar = jnp.broadcast_to(a8[:, :S], (SUBLANES, S))
    ai = jnp.broadcast_to(a8[:, S:], (SUBLANES, S))
    er = jnp.zeros((SUBLANES, S), F32)
    ei = jnp.zeros((SUBLANES, S), F32)
    for j in range(nseg):
        rows = slice(j * SUBLANES, (j + 1) * SUBLANES)
        e_scr[rows, :S] = er
        e_scr[rows, S:] = ei
        zr = r_scr[rows, CW:CW + S]
        zi = r_scr[rows, CW + S:]
        er, ei = ar * er - ai * ei + zr, ar * ei + ai * er + zi
    ein_scr[:, :S] = er
    ein_scr[:, S:] = ei

    pw = pw_ref[0]
    pnr = pw[nseg:nseg + 1, :S]
    pni = pw[nseg:nseg + 1, S:]
    cr = carry_scr[0:1, :S]
    ci = carry_scr[0:1, S:]
    for s in range(SUBLANES):
        fr = ein_scr[s:s + 1, :S]
        fi = ein_scr[s:s + 1, S:]
        ein_scr[s:s + 1, :S] = cr
        ein_scr[s:s + 1, S:] = ci
        cr, ci = fr + pnr * cr - pni * ci, fi + pnr * ci + pni * cr
    carry_scr[0:1, :S] = cr
    carry_scr[0:1, S:] = ci

    einr = ein_scr[:, :S]
    eini = ein_scr[:, S:]
    for j in range(nseg):
        rows = slice(j * SUBLANES, (j + 1) * SUBLANES)
        pr = pw[j:j + 1, :S]
        pi = pw[j:j + 1, S:]
        e_scr[rows, :S] = e_scr[rows, :S] + (pr * einr - pi * eini)
        e_scr[rows, S:] = e_scr[rows, S:] + (pr * eini + pi * einr)

    r_scr[:, :CW] = r_scr[:, :CW] + _dot(e_scr[...].astype(BF16), wc_ref[0])

    d = d_ref[...]
    for j in range(nseg):
        rows = slice(j * SUBLANES, (j + 1) * SUBLANES)
        for i in range(CHUNK):
            cols = slice(i * LANES, (i + 1) * LANES)
            y_ref[pl.ds(CHUNK * j + i, SUBLANES, stride=seg_stride), :] = (
                r_scr[rows, cols] + d * lhs_scr[rows, cols])


def _s5(u, w1, wc, a8, pw, d, *, batch, seq, tm):
    T, ssm_w = u.shape
    nseg = tm // (SUBLANES * CHUNK)
    nc = tm // CHUNK
    CW = CHUNK * LANES
    S2 = 2 * OCT_STATE
    n_oct = ssm_w // LANES
    tiles = seq // tm
    return pl.pallas_call(
        functools.partial(_s5_kernel, tm=tm, nseg=nseg),
        grid=(n_oct, batch, tiles),
        in_specs=[
            pl.BlockSpec((tm, LANES), lambda o, b, t: (b * tiles + t, o)),
            pl.BlockSpec((1, CW, CW + S2), lambda o, b, t: (o, 0, 0)),
            pl.BlockSpec((1, S2, CW), lambda o, b, t: (o, 0, 0)),
            pl.BlockSpec((1, 1, S2), lambda o, b, t: (o, 0, 0)),
            pl.BlockSpec((1, nseg + 1, S2), lambda o, b, t: (o, 0, 0)),
            pl.BlockSpec((1, LANES), lambda o, b, t: (0, o)),
        ],
        out_specs=pl.BlockSpec((tm, LANES), lambda o, b, t: (b * tiles + t, o)),
        out_shape=jax.ShapeDtypeStruct((T, ssm_w), F32),
        scratch_shapes=[
            pltpu.VMEM((nc, CW), F32),
            pltpu.VMEM((nc, CW + S2), F32),
            pltpu.VMEM((nc, S2), F32),
            pltpu.VMEM((SUBLANES, S2), F32),
            pltpu.VMEM((SUBLANES, S2), F32),
        ],
        compiler_params=pltpu.CompilerParams(
            dimension_semantics=("arbitrary", "arbitrary", "arbitrary"), vmem_limit_bytes=VMEM_LIMIT_BYTES),
        name="s5_scan",
    )(u, w1, wc, a8, pw, d)


def _s5_tables(a_re, a_im, b_re, b_im, c_re, c_im, log_dt, nseg):
    hp = lax.Precision.HIGHEST
    G, P = a_re.shape
    H = b_re.shape[-1]
    n_oct = G // OCTET
    dt = jnp.exp(log_dt)[:, None]
    lam_r = a_re * dt
    lam_i = a_im * dt

    def powers(n):
        n = n.astype(F32)[:, None, None]
        mag = jnp.exp(lam_r * n)
        return mag * jnp.cos(lam_i * n), mag * jnp.sin(lam_i * n)

    pr, pi = powers(jnp.arange(CHUNK + 1))
    den = a_re * a_re + a_im * a_im
    nr = pr[1] - 1.0
    ni = pi[1]
    fr = ((nr * a_re + ni * a_im) / den)[..., None]
    fi = ((ni * a_re - nr * a_im) / den)[..., None]
    bbr = fr * b_re - fi * b_im
    bbi = fr * b_im + fi * b_re
    vr = pr[:CHUNK, :, :, None] * bbr - pi[:CHUNK, :, :, None] * bbi
    vi = pr[:CHUNK, :, :, None] * bbi + pi[:CHUNK, :, :, None] * bbr
    kern = (jnp.einsum('gkp,tgph->tgkh', c_re, vr, precision=hp)
            - jnp.einsum('gkp,tgph->tgkh', c_im, vi, precision=hp))
    eye = jnp.eye(OCTET, dtype=F32)[None, None, :, None, None, :, None]

    ii = jnp.arange(CHUNK)[:, None]
    jj = jnp.arange(CHUNK)[None, :]
    tau = jj - ii
    kt = kern[jnp.clip(tau, 0, CHUNK - 1)] * (tau >= 0).astype(F32)[:, :, None, None, None]
    kt = kt.reshape(CHUNK, CHUNK, n_oct, OCTET, H, H).transpose(2, 0, 3, 5, 1, 4)
    w_intra = (kt[:, :, :, :, :, None, :] * eye).reshape(n_oct, CHUNK * LANES, CHUNK * LANES)

    vb = jnp.stack([vr[::-1], vi[::-1]], axis=1)
    vb = vb.reshape(CHUNK, 2, n_oct, OCTET, P, H).transpose(2, 0, 3, 5, 1, 4)
    w_b = (vb[:, :, :, :, :, None, :] * eye).reshape(n_oct, CHUNK * LANES, 2 * OCT_STATE)

    pjr = pr[1:, :, None, :]
    pji = pi[1:, :, None, :]
    wr = c_re[None] * pjr - c_im[None] * pji
    wi = -(c_re[None] * pji + c_im[None] * pjr)
    wcm = jnp.stack([wr, wi], axis=0)
    wcm = wcm.reshape(2, CHUNK, n_oct, OCTET, H, P).transpose(2, 0, 3, 5, 1, 4)
    w_c = (wcm[:, :, :, :, :, None, :] * eye).reshape(n_oct, 2 * OCT_STATE, CHUNK * LANES)

    w1 = jnp.concatenate([w_intra, w_b], axis=2).astype(BF16)
    a8 = jnp.concatenate([pr[CHUNK].reshape(n_oct, 1, OCT_STATE), pi[CHUNK].reshape(n_oct, 1, OCT_STATE)], axis=2)
    qr, qi = powers(CHUNK * jnp.arange(nseg + 1))
    pw = jnp.concatenate([qr.reshape(nseg + 1, n_oct, OCT_STATE), qi.reshape(nseg + 1, n_oct, OCT_STATE)], axis=2)
    return w1, w_c.astype(BF16), a8, pw.transpose(1, 0, 2)


def _moba_kernel(q_ref, k_ref, vt_ref, km_ref, o_ref, qt_scr, bias_scr, m_scr, l_scr, acc_scr, *, nblocks):
    qb = pl.program_id(2)
    BQ = MOBA_BLOCK
    q2t = q_ref[...].astype(F32).T
    feat = lax.broadcasted_iota(jnp.int32, q2t.shape, 0)
    km = km_ref[0]
    km_hi = km.astype(BF16)
    km_lo = (km - km_hi.astype(F32)).astype(BF16)
    blk = lax.broadcasted_iota(jnp.int32, (nblocks, BQ), 0)
    blk_f = blk.astype(F32)
    kpos = lax.broadcasted_iota(jnp.int32, (BQ, BQ), 0)
    qpos = lax.broadcasted_iota(jnp.int32, (BQ, BQ), 1)
    k_own = k_ref[pl.ds(pl.multiple_of(qb * BQ, BQ), BQ), :]
    vt_own = vt_ref[qb]

    for a in range(2):
        rows = slice(a * HEAD_DIM, (a + 1) * HEAD_DIM)
        qat = jnp.where((feat >= a * HEAD_DIM) & (feat < (a + 1) * HEAD_DIM), q2t, 0.0).astype(BF16)
        qt_scr[a] = qat
        gate = _dot(km_hi, qat) + _dot(km_lo, qat)
        gate = jnp.where(blk < qb, gate, NEG_INF)
        sel = jnp.zeros(gate.shape, jnp.bool_)
        for _ in range(MOBA_TOPK):
            top = jnp.max(gate, axis=0, keepdims=True)
            idx = jnp.min(jnp.where(gate == top, blk_f, float(nblocks)), axis=0, keepdims=True)
            pick = blk_f == idx
            sel = sel | pick
            gate = jnp.where(pick, -jnp.inf, gate)
        bias_scr[a] = jnp.where(sel & (blk < qb), 0.0, NEG_INF)
        s = jnp.where(kpos <= qpos, _dot(k_own, qat), NEG_INF)
        m = jnp.max(s, axis=0, keepdims=True)
        p = jnp.exp(s - m)
        m_scr[a] = m
        l_scr[a] = jnp.sum(p, axis=0, keepdims=True)
        acc_scr[a] = _dot(vt_own, p.astype(BF16))[rows]

    def body(n, carry):
        kb = k_ref[pl.ds(pl.multiple_of(n * BQ, BQ), BQ), :]
        vtb = vt_ref[n]
        for a in range(2):
            rows = slice(a * HEAD_DIM, (a + 1) * HEAD_DIM)
            s = _dot(kb, qt_scr[a]) + bias_scr[a, pl.ds(n, 1), :]
            m_old = m_scr[a]
            m_new = jnp.maximum(m_old, jnp.max(s, axis=0, keepdims=True))
            alpha = jnp.exp(m_old - m_new)
            p = jnp.exp(s - m_new)
            l_scr[a] = alpha * l_scr[a] + jnp.sum(p, axis=0, keepdims=True)
            acc_scr[a] = alpha * acc_scr[a] + _dot(vtb, p.astype(BF16))[rows]
            m_scr[a] = m_new
        return carry

    lax.fori_loop(0, qb, body, 0)
    ot = jnp.concatenate([acc_scr[0] / l_scr[0], acc_scr[1] / l_scr[1]], axis=0)
    o_ref[...] = ot.T.astype(BF16)


def _moba(q, k, vt, kmean, *, batch, seq):
    T, attn_w = q.shape
    nblocks = seq // MOBA_BLOCK
    n_pairs = attn_w // LANES
    BQ = MOBA_BLOCK
    return pl.pallas_call(
        functools.partial(_moba_kernel, nblocks=nblocks),
        grid=(batch, n_pairs, nblocks),
        in_specs=[
            pl.BlockSpec((BQ, LANES), lambda b, h, i: (b * nblocks + i, h)),
            pl.BlockSpec((seq, LANES), lambda b, h, i: (b, h)),
            pl.BlockSpec((nblocks, LANES, BQ), lambda b, h, i: (b, h, 0)),
            pl.BlockSpec((1, nblocks, LANES), lambda b, h, i: (b, 0, h)),
        ],
        out_specs=pl.BlockSpec((BQ, LANES), lambda b, h, i: (b * nblocks + i, h)),
        out_shape=jax.ShapeDtypeStruct((T, attn_w), BF16),
        scratch_shapes=[
            pltpu.VMEM((2, LANES, BQ), BF16),
            pltpu.VMEM((2, nblocks, BQ), F32),
            pltpu.VMEM((2, 1, BQ), F32),
            pltpu.VMEM((2, 1, BQ), F32),
            pltpu.VMEM((2, HEAD_DIM, BQ), F32),
        ],
        compiler_params=pltpu.CompilerParams(
            dimension_semantics=("arbitrary", "arbitrary", "arbitrary"), vmem_limit_bytes=VMEM_LIMIT_BYTES),
        name="moba_attn",
    )(q, k, vt, kmean)


def _out_ffn_kernel(x1_ref, y_ref, a_ref, gm_ref, wgate_ref, gluw_ref, glub_ref, wbs_ref, wba_ref, wout_ref,
                    g3_ref, wg_ref, wu_ref, wd_ref, gf_ref, o_ref):
    x1 = x1_ref[...]
    D = x1.shape[1]
    h = _rmsnorm(x1, gm_ref[...]).astype(BF16)
    gates = _dot(h, wgate_ref[...])
    ys = jax.nn.gelu(y_ref[...])
    ys = ys * jax.nn.sigmoid(_dot(ys.astype(BF16), gluw_ref[...]) + glub_ref[...])
    branch_a = _dot(ys.astype(BF16), wbs_ref[...])
    branch_b = _dot(a_ref[...], wba_ref[...])
    merged = jax.nn.sigmoid(gates[:, :D]) * branch_a + jax.nn.sigmoid(gates[:, D:]) * branch_b
    x2 = x1 + _dot(merged.astype(BF16), wout_ref[...])

    h3 = _rmsnorm(x2, g3_ref[...]).astype(BF16)
    act = (jax.nn.silu(_dot(h3, wg_ref[...])) * _dot(h3, wu_ref[...])).astype(BF16)
    x3 = x2 + 0.5 * _dot(act, wd_ref[...])
    o_ref[...] = _rmsnorm(x3, gf_ref[...])


def _out_ffn(x1, y, attn, gm, wgate, gluw, glub, wbs, wba, wout, g3, wg, wu, wd, gf, *, tm):
    T, D = x1.shape
    F = wg.shape[1]
    ssm_w = y.shape[1]
    attn_w = attn.shape[1]
    row = lambda i: (i, 0)
    return pl.pallas_call(
        _out_ffn_kernel,
        grid=(T // tm,),
        in_specs=[
            pl.BlockSpec((tm, D), row), pl.BlockSpec((tm, ssm_w), row), pl.BlockSpec((tm, attn_w), row),
            _const_spec((1, D)), _const_spec((D, 2 * D)), _const_spec((ssm_w, ssm_w)), _const_spec((1, ssm_w)),
            _const_spec((ssm_w, D)), _const_spec((attn_w, D)), _const_spec((D, D)),
            _const_spec((1, D)), _const_spec((D, F)), _const_spec((D, F)), _const_spec((F, D)),
            _const_spec((1, D)),
        ],
        out_specs=pl.BlockSpec((tm, D), row),
        out_shape=jax.ShapeDtypeStruct((T, D), F32),
        compiler_params=pltpu.CompilerParams(
            dimension_semantics=("arbitrary",), vmem_limit_bytes=VMEM_LIMIT_BYTES),
        name="out_ffn2",
    )(x1, y, attn, gm, wgate, gluw, glub, wbs, wba, wout, g3, wg, wu, wd, gf)


def _rope_tables(seq):
    pos = jnp.arange(seq, dtype=F32)
    inv_freq = ROPE_THETA ** (-jnp.arange(0, HEAD_DIM, 2, dtype=F32) / HEAD_DIM)
    ang = pos[:, None] * inv_freq[None, :]
    cos = jnp.cos(ang)
    sin = jnp.sin(ang)
    cos_t = jnp.tile(cos, (1, 2 * LANES // HEAD_DIM))
    sin_t = jnp.tile(jnp.concatenate([-sin, sin], axis=1), (1, LANES // HEAD_DIM))
    return cos_t, sin_t


TM_FFN = 256
TM_S5 = 2048


def kernel(x, ffn1_norm, ffn1_w_gate, ffn1_w_up, ffn1_w_down, mix_norm, w_in, ssm_a_re, ssm_a_im, ssm_b_re, ssm_b_im, ssm_c_re, ssm_c_im, ssm_d, ssm_log_dt, glu_w, glu_b, w_branch_ssm, w_branch_attn, w_out, ffn2_norm, ffn2_w_gate, ffn2_w_up, ffn2_w_down, final_norm):
    B, L, D = x.shape
    ssm_w = glu_w.shape[1]
    attn_w = w_branch_attn.shape[1]
    T = B * L
    assert ffn1_norm.shape[0] == 1, "single-layer trunk only"
    assert attn_w == N_HEADS * HEAD_DIM and L % TM_S5 == 0 and T % TM_FFN == 0
    assert TM_FFN % MOBA_BLOCK == 0 and L % TM_FFN == 0
    bf = lambda w: w[0].astype(BF16)
    row = lambda v: v[0][None]
    cos_t, sin_t = _rope_tables(L)
    nseg = TM_S5 // (SUBLANES * CHUNK)
    wi = w_in[0]
    off_v = ssm_w + 2 * attn_w
    off_g = ssm_w + 3 * attn_w
    x1, u, q, k, vt, kmean8 = _ffn_inproj(
        x.reshape(T, D), row(ffn1_norm), bf(ffn1_w_gate), bf(ffn1_w_up), bf(ffn1_w_down), row(mix_norm),
        wi[:, :off_v].astype(BF16), wi[:, off_v:off_g].T.astype(BF16), cos_t, sin_t, seq=L, tm=TM_FFN)
    w1, wc, a8, pw = _s5_tables(ssm_a_re[0], ssm_a_im[0], ssm_b_re[0], ssm_b_im[0],
                                ssm_c_re[0], ssm_c_im[0], ssm_log_dt[0], nseg)
    y = _s5(u, w1, wc, a8, pw, row(ssm_d), batch=B, seq=L, tm=TM_S5)
    kmean = kmean8[::SUBLANES].reshape(B, L // MOBA_BLOCK, attn_w)
    attn = _moba(q, k, vt, kmean, batch=B, seq=L)
    out = _out_ffn(
        x1, y, attn, row(mix_norm), wi[:, off_g:].astype(BF16), bf(glu_w), row(glu_b), bf(w_branch_ssm),
        bf(w_branch_attn), bf(w_out), row(ffn2_norm), bf(ffn2_w_gate), bf(ffn2_w_up), bf(ffn2_w_down),
        final_norm[None], tm=TM_FFN)
    return out.reshape(B, L, D)
```

```python
import functools
import math

import jax
import jax.numpy as jnp
from jax import lax
from jax.experimental import pallas as pl
from jax.experimental.pallas import tpu as pltpu

F32 = jnp.float32
BF16 = jnp.bfloat16

N_HEADS = 8
HEAD_DIM = 64
MOBA_BLOCK = 256
MOBA_TOPK = 3
SSM_GROUP = 16
SSM_STATE = 64
ROPE_THETA = 10000.0
RMS_EPS = 1e-6
NEG_INF = -1e30

LANES = 128
SUBLANES = 8
VMEM_LIMIT_BYTES = 56 * 1024 * 1024

CHUNK = 8
OCTET = LANES // SSM_GROUP
OCT_STATE = OCTET * SSM_STATE


def _rmsnorm(x, gain):
    inv = lax.rsqrt(jnp.mean(x * x, axis=-1, keepdims=True) + RMS_EPS)
    return (x * inv) * gain


def _dot(a, b):
    return jnp.dot(a, b, preferred_element_type=F32)


def _const_spec(shape):
    nd = len(shape)
    return pl.BlockSpec(shape, lambda *_: (0,) * nd, pipeline_mode=pl.Buffered(1))


def _rotary_tile(x, cos, sin_signed, first_half):
    swapped = jnp.where(first_half, pltpu.roll(x, LANES - HEAD_DIM // 2, 1), pltpu.roll(x, HEAD_DIM // 2, 1))
    return x * cos + swapped * sin_signed


def _ffn_inproj_kernel(x_ref, g1_ref, wg_ref, wu_ref, wd_ref, g2_ref, win_ref, wvt_ref, cos_ref, sin_ref,
                       x1_ref, u_ref, q_ref, k_ref, vt_ref, kmean_ref, *, attn_w, ssm_w, nblocks):
    x = x_ref[...]
    h = _rmsnorm(x, g1_ref[...]).astype(BF16)
    act = (jax.nn.silu(_dot(h, wg_ref[...])) * _dot(h, wu_ref[...])).astype(BF16)
    x1 = x + 0.5 * _dot(act, wd_ref[...])
    x1_ref[...] = x1

    h2 = _rmsnorm(x1, g2_ref[...]).astype(BF16)
    proj = _dot(h2, win_ref[...])
    u_ref[...] = proj[:, :ssm_w]

    cos = cos_ref[...]
    sin_signed = sin_ref[...]
    lane = lax.broadcasted_iota(jnp.int32, cos.shape, 1)
    first_half = (lane % HEAD_DIM) < (HEAD_DIM // 2)
    scale = HEAD_DIM ** -0.5
    tm = x.shape[0]
    nblk = tm // MOBA_BLOCK
    key_blk = (pl.program_id(0) * nblk + lax.broadcasted_iota(jnp.int32, cos.shape, 0) // MOBA_BLOCK) % nblocks
    blk_onehot = (lane == key_blk).astype(BF16)
    for t in range(attn_w // LANES):
        qs = proj[:, ssm_w + t * LANES: ssm_w + (t + 1) * LANES]
        ks = proj[:, ssm_w + attn_w + t * LANES: ssm_w + attn_w + (t + 1) * LANES]
        q_ref[:, t * LANES:(t + 1) * LANES] = (_rotary_tile(qs, cos, sin_signed, first_half) * scale).astype(BF16)
        kr = _rotary_tile(ks, cos, sin_signed, first_half)
        k_ref[:, 2 * t * LANES:(2 * t + 1) * LANES] = kr.astype(BF16)
        k_ref[:, (2 * t + 1) * LANES:(2 * t + 2) * LANES] = blk_onehot
        for j in range(nblk):
            mean = jnp.mean(kr[j * MOBA_BLOCK:(j + 1) * MOBA_BLOCK], axis=0, keepdims=True)
            kmean_ref[j * SUBLANES:(j + 1) * SUBLANES, t * LANES:(t + 1) * LANES] = jnp.broadcast_to(
                mean, (SUBLANES, LANES))

    vt = lax.dot_general(wvt_ref[...], h2, (((1,), (1,)), ((), ())), preferred_element_type=F32).astype(BF16)
    ones = jnp.ones((HEAD_DIM, MOBA_BLOCK), BF16)
    for j in range(nblk):
        for hd in range(attn_w // HEAD_DIM):
            vt_ref[j, 2 * hd * HEAD_DIM:(2 * hd + 1) * HEAD_DIM, :] = (
                vt[hd * HEAD_DIM:(hd + 1) * HEAD_DIM, j * MOBA_BLOCK:(j + 1) * MOBA_BLOCK])
            vt_ref[j, (2 * hd + 1) * HEAD_DIM:(2 * hd + 2) * HEAD_DIM, :] = ones


def _ffn_inproj(x2d, g1, wg, wu, wd, g2, win_uqk, wv_t, cos_t, sin_t, *, seq, tm):
    T, D = x2d.shape
    F = wg.shape[1]
    attn_w = wv_t.shape[0]
    ssm_w = win_uqk.shape[1] - 2 * attn_w
    nblk = tm // MOBA_BLOCK
    tiles_per_seq = seq // tm
    row = lambda i: (i, 0)
    return pl.pallas_call(
        functools.partial(_ffn_inproj_kernel, attn_w=attn_w, ssm_w=ssm_w, nblocks=seq // MOBA_BLOCK),
        grid=(T // tm,),
        in_specs=[
            pl.BlockSpec((tm, D), row),
            _const_spec((1, D)), _const_spec((D, F)), _const_spec((D, F)), _const_spec((F, D)),
            _const_spec((1, D)), _const_spec((D, ssm_w + 2 * attn_w)), _const_spec((attn_w, D)),
            pl.BlockSpec((tm, LANES), lambda i: (i % tiles_per_seq, 0)),
            pl.BlockSpec((tm, LANES), lambda i: (i % tiles_per_seq, 0)),
        ],
        out_specs=[
            pl.BlockSpec((tm, D), row),
            pl.BlockSpec((tm, ssm_w), row),
            pl.BlockSpec((tm, attn_w), row),
            pl.BlockSpec((tm, 2 * attn_w), row),
            pl.BlockSpec((nblk, 2 * attn_w, MOBA_BLOCK), lambda i: (i, 0, 0)),
            pl.BlockSpec((nblk * SUBLANES, attn_w), row),
        ],
        out_shape=[
            jax.ShapeDtypeStruct((T, D), F32),
            jax.ShapeDtypeStruct((T, ssm_w), F32),
            jax.ShapeDtypeStruct((T, attn_w), BF16),
            jax.ShapeDtypeStruct((T, 2 * attn_w), BF16),
            jax.ShapeDtypeStruct((T // MOBA_BLOCK, 2 * attn_w, MOBA_BLOCK), BF16),
            jax.ShapeDtypeStruct((T // MOBA_BLOCK * SUBLANES, attn_w), F32),
        ],
        compiler_params=pltpu.CompilerParams(
            dimension_semantics=("arbitrary",), vmem_limit_bytes=VMEM_LIMIT_BYTES),
        name="ffn1_inproj",
    )(x2d, g1, wg, wu, wd, g2, win_uqk, wv_t, cos_t, sin_t)


def _s5_kernel(u_ref, w1_ref, wc_ref, a8_ref, pw_ref, d_ref, y_ref,
               lhs_scr, r_scr, e_scr, ein_scr, carry_scr, *, tm, nseg):
    S = OCT_STATE
    CW = CHUNK * LANES
    seg_stride = tm // SUBLANES

    @pl.when(pl.program_id(2) == 0)
    def _():
        carry_scr[...] = jnp.zeros_like(carry_scr)

    for j in range(nseg):
        for i in range(CHUNK):
            lhs_scr[j * SUBLANES:(j + 1) * SUBLANES, i * LANES:(i + 1) * LANES] = (
                u_ref[pl.ds(CHUNK * j + i, SUBLANES, stride=seg_stride), :])

    r_scr[...] = _dot(lhs_scr[...].astype(BF16), w1_ref[0])

    a8 = a8_ref[0]
    ar = jnp.broadcast_to(a8[:, :S], (SUBLANES, S))
    ai = jnp.broadcast_to(a8[:, S:], (SUBLANES, S))
    er = jnp.zeros((SUBLANES, S), F32)
    ei = jnp.zeros((SUBLANES, S), F32)
    for j in range(nseg):
        rows = slice(j * SUBLANES, (j + 1) * SUBLANES)
        e_scr[rows, :S] = er
        e_scr[rows, S:] = ei
        zr = r_scr[rows, CW:CW + S]
        zi = r_scr[rows, CW + S:]
        er, ei = ar * er - ai * ei + zr, ar * ei + ai * er + zi
    ein_scr[:, :S] = er
    ein_scr[:, S:] = ei

    pw = pw_ref[0]
    pnr = pw[nseg:nseg + 1, :S]
    pni = pw[nseg:nseg + 1, S:]
    cr = carry_scr[0:1, :S]
    ci = carry_scr[0:1, S:]
    for s in range(SUBLANES):
        fr = ein_scr[s:s + 1, :S]
        fi = ein_scr[s:s + 1, S:]
        ein_scr[s:s + 1, :S] = cr
        ein_scr[s:s + 1, S:] = ci
        cr, ci = fr + pnr * cr - pni * ci, fi + pnr * ci + pni * cr
    carry_scr[0:1, :S] = cr
    carry_scr[0:1, S:] = ci

    einr = ein_scr[:, :S]
    eini = ein_scr[:, S:]
    for j in range(nseg):
        rows = slice(j * SUBLANES, (j + 1) * SUBLANES)
        pr = pw[j:j + 1, :S]
        pi = pw[j:j + 1, S:]
        e_scr[rows, :S] = e_scr[rows, :S] + (pr * einr - pi * eini)
        e_scr[rows, S:] = e_scr[rows, S:] + (pr * eini + pi * einr)

    r_scr[:, :CW] = r_scr[:, :CW] + _dot(e_scr[...].astype(BF16), wc_ref[0])

    d = d_ref[...]
    for j in range(nseg):
        rows = slice(j * SUBLANES, (j + 1) * SUBLANES)
        for i in range(CHUNK):
            cols = slice(i * LANES, (i + 1) * LANES)
            y_ref[pl.ds(CHUNK * j + i, SUBLANES, stride=seg_stride), :] = (
                r_scr[rows, cols] + d * lhs_scr[rows, cols])


def _s5(u, w1, wc, a8, pw, d, *, batch, seq, tm):
    T, ssm_w = u.shape
    nseg = tm // (SUBLANES * CHUNK)
    nc = tm // CHUNK
    CW = CHUNK * LANES
    S2 = 2 * OCT_STATE
    n_oct = ssm_w // LANES
    tiles = seq // tm
    return pl.pallas_call(
        functools.partial(_s5_kernel, tm=tm, nseg=nseg),
        grid=(n_oct, batch, tiles),
        in_specs=[
            pl.BlockSpec((tm, LANES), lambda o, b, t: (b * tiles + t, o)),
            pl.BlockSpec((1, CW, CW + S2), lambda o, b, t: (o, 0, 0)),
            pl.BlockSpec((1, S2, CW), lambda o, b, t: (o, 0, 0)),
            pl.BlockSpec((1, 1, S2), lambda o, b, t: (o, 0, 0)),
            pl.BlockSpec((1, nseg + 1, S2), lambda o, b, t: (o, 0, 0)),
            pl.BlockSpec((1, LANES), lambda o, b, t: (0, o)),
        ],
        out_specs=pl.BlockSpec((tm, LANES), lambda o, b, t: (b * tiles + t, o)),
        out_shape=jax.ShapeDtypeStruct((T, ssm_w), F32),
        scratch_shapes=[
            pltpu.VMEM((nc, CW), F32),
            pltpu.VMEM((nc, CW + S2), F32),
            pltpu.VMEM((nc, S2), F32),
            pltpu.VMEM((SUBLANES, S2), F32),
            pltpu.VMEM((SUBLANES, S2), F32),
        ],
        compiler_params=pltpu.CompilerParams(
            dimension_semantics=("arbitrary", "arbitrary", "arbitrary"), vmem_limit_bytes=VMEM_LIMIT_BYTES),
        name="s5_scan",
    )(u, w1, wc, a8, pw, d)


def _s5_tables(a_re, a_im, b_re, b_im, c_re, c_im, log_dt, nseg):
    hp = lax.Precision.HIGHEST
    G, P = a_re.shape
    H = b_re.shape[-1]
    n_oct = G // OCTET
    dt = jnp.exp(log_dt)[:, None]
    lam_r = a_re * dt
    lam_i = a_im * dt

    def powers(n):
        n = n.astype(F32)[:, None, None]
        mag = jnp.exp(lam_r * n)
        return mag * jnp.cos(lam_i * n), mag * jnp.sin(lam_i * n)

    pr, pi = powers(jnp.arange(CHUNK + 1))
    den = a_re * a_re + a_im * a_im
    nr = pr[1] - 1.0
    ni = pi[1]
    fr = ((nr * a_re + ni * a_im) / den)[..., None]
    fi = ((ni * a_re - nr * a_im) / den)[..., None]
    bbr = fr * b_re - fi * b_im
    bbi = fr * b_im + fi * b_re
    vr = pr[:CHUNK, :, :, None] * bbr - pi[:CHUNK, :, :, None] * bbi
    vi = pr[:CHUNK, :, :, None] * bbi + pi[:CHUNK, :, :, None] * bbr
    kern = (jnp.einsum('gkp,tgph->tgkh', c_re, vr, precision=hp)
            - jnp.einsum('gkp,tgph->tgkh', c_im, vi, precision=hp))
    eye = jnp.eye(OCTET, dtype=F32)[None, None, :, None, None, :, None]

    ii = jnp.arange(CHUNK)[:, None]
    jj = jnp.arange(CHUNK)[None, :]
    tau = jj - ii
    kt = kern[jnp.clip(tau, 0, CHUNK - 1)] * (tau >= 0).astype(F32)[:, :, None, None, None]
    kt = kt.reshape(CHUNK, CHUNK, n_oct, OCTET, H, H).transpose(2, 0, 3, 5, 1, 4)
    w_intra = (kt[:, :, :, :, :, None, :] * eye).reshape(n_oct, CHUNK * LANES, CHUNK * LANES)

    vb = jnp.stack([vr[::-1], vi[::-1]], axis=1)
    vb = vb.reshape(CHUNK, 2, n_oct, OCTET, P, H).transpose(2, 0, 3, 5, 1, 4)
    w_b = (vb[:, :, :, :, :, None, :] * eye).reshape(n_oct, CHUNK * LANES, 2 * OCT_STATE)

    pjr = pr[1:, :, None, :]
    pji = pi[1:, :, None, :]
    wr = c_re[None] * pjr - c_im[None] * pji
    wi = -(c_re[None] * pji + c_im[None] * pjr)
    wcm = jnp.stack([wr, wi], axis=0)
    wcm = wcm.reshape(2, CHUNK, n_oct, OCTET, H, P).transpose(2, 0, 3, 5, 1, 4)
    w_c = (wcm[:, :, :, :, :, None, :] * eye).reshape(n_oct, 2 * OCT_STATE, CHUNK * LANES)

    w1 = jnp.concatenate([w_intra, w_b], axis=2).astype(BF16)
    a8 = jnp.concatenate([pr[CHUNK].reshape(n_oct, 1, OCT_STATE), pi[CHUNK].reshape(n_oct, 1, OCT_STATE)], axis=2)
    qr, qi = powers(CHUNK * jnp.arange(nseg + 1))
    pw = jnp.concatenate([qr.reshape(nseg + 1, n_oct, OCT_STATE), qi.reshape(nseg + 1, n_oct, OCT_STATE)], axis=2)
    return w1, w_c.astype(BF16), a8, pw.transpose(1, 0, 2)


KV_BLOCKS_PER_STEP = 2


def _moba_kernel(q_ref, k_ref, vt_ref, km_ref, o_ref, qt_scr, s0_scr, s1_scr, m_scr, acc_scr, *, nblocks):
    t = pl.program_id(2)
    BK = MOBA_BLOCK
    NB = KV_BLOCKS_PER_STEP
    QT = q_ref.shape[0]
    q2t = q_ref[...].astype(F32).T
    feat = lax.broadcasted_iota(jnp.int32, q2t.shape, 0)
    km = km_ref[0]
    km_hi = km.astype(BF16)
    km_lo = (km - km_hi.astype(F32)).astype(BF16)
    blk = lax.broadcasted_iota(jnp.int32, (nblocks, QT), 0)
    blk_f = blk.astype(F32)
    own = t * (QT // BK) + lax.broadcasted_iota(jnp.int32, (nblocks, QT), 1) // BK
    pad = jnp.zeros((LANES - nblocks, QT), BF16)

    for a in range(2):
        qat = jnp.where((feat >= a * HEAD_DIM) & (feat < (a + 1) * HEAD_DIM), q2t, 0.0).astype(BF16)
        gate = _dot(km_hi, qat) + _dot(km_lo, qat)
        gate = jnp.where(blk < own, gate, NEG_INF)
        sel = jnp.zeros(gate.shape, jnp.bool_)
        for _ in range(MOBA_TOPK):
            top = jnp.max(gate, axis=0, keepdims=True)
            idx = jnp.min(jnp.where(gate == top, blk_f, float(nblocks)), axis=0, keepdims=True)
            pick = blk_f == idx
            sel = sel | pick
            gate = jnp.where(pick, -jnp.inf, gate)
        bias = jnp.where((sel & (blk < own)) | (blk == own), 0.0, NEG_INF).astype(BF16)
        qt_scr[a] = jnp.concatenate([qat, bias, pad], axis=0)
        m_scr[a] = jnp.full((1, QT), NEG_INF, F32)
        acc_scr[a] = jnp.zeros((LANES, QT), F32)

    def scores(c, s_scr):
        kb = k_ref[pl.ds(pl.multiple_of(c * (NB * BK), NB * BK), NB * BK), :]
        for a in range(2):
            s_scr[a] = _dot(kb, qt_scr[a])

    def consume(c, s_scr):
        for a in range(2):
            vta = jnp.concatenate([vt_ref[c * NB + j, a * LANES:(a + 1) * LANES, :] for j in range(NB)], axis=1)
            s = s_scr[a]
            m_old = m_scr[a]
            m_new = jnp.maximum(m_old, jnp.max(s, axis=0, keepdims=True))
            p = jnp.exp(s - m_new).astype(BF16)
            acc_scr[a] = jnp.exp(m_old - m_new) * acc_scr[a] + _dot(vta, p)
            m_scr[a] = m_new

    tri = (lax.broadcasted_iota(jnp.int32, (BK, BK), 0) <= lax.broadcasted_iota(jnp.int32, (BK, BK), 1))

    def causal_patch(e, s_scr):
        for a in range(2):
            for j in range(NB):
                rows = slice(j * BK, (j + 1) * BK)
                cols = slice((e * NB + j) * BK, (e * NB + j + 1) * BK)
                s_scr[a, rows, cols] = jnp.where(tri, s_scr[a, rows, cols], NEG_INF)

    scores(0, s0_scr)

    def body(i, carry):
        scores(2 * i + 1, s1_scr)
        consume(2 * i, s0_scr)
        scores(2 * i + 2, s0_scr)
        consume(2 * i + 1, s1_scr)
        return carry

    lax.fori_loop(0, t, body, 0)
    scores(2 * t + 1, s1_scr)
    causal_patch(0, s0_scr)
    consume(2 * t, s0_scr)
    causal_patch(1, s1_scr)
    consume(2 * t + 1, s1_scr)
    ot = jnp.concatenate([acc_scr[a, :HEAD_DIM] / acc_scr[a, HEAD_DIM:HEAD_DIM + 1] for a in range(2)], axis=0)
    o_ref[...] = ot.T.astype(BF16)


def _moba(q, k, vt, kmean, *, batch, seq):
    T, attn_w = q.shape
    nblocks = seq // MOBA_BLOCK
    n_pairs = attn_w // LANES
    BK = MOBA_BLOCK
    QT = 2 * KV_BLOCKS_PER_STEP * BK
    tiles = seq // QT
    return pl.pallas_call(
        functools.partial(_moba_kernel, nblocks=nblocks),
        grid=(batch, n_pairs, tiles),
        in_specs=[
            pl.BlockSpec((QT, LANES), lambda b, h, i: (b * tiles + i, h)),
            pl.BlockSpec((seq, 2 * LANES), lambda b, h, i: (b, h)),
            pl.BlockSpec((nblocks, 2 * LANES, BK), lambda b, h, i: (b, h, 0)),
            pl.BlockSpec((1, nblocks, LANES), lambda b, h, i: (b, 0, h)),
        ],
        out_specs=pl.BlockSpec((QT, LANES), lambda b, h, i: (b * tiles + i, h)),
        out_shape=jax.ShapeDtypeStruct((T, attn_w), BF16),
        scratch_shapes=[
            pltpu.VMEM((2, 2 * LANES, QT), BF16),
            pltpu.VMEM((2, KV_BLOCKS_PER_STEP * BK, QT), F32),
            pltpu.VMEM((2, KV_BLOCKS_PER_STEP * BK, QT), F32),
            pltpu.VMEM((2, 1, QT), F32),
            pltpu.VMEM((2, LANES, QT), F32),
        ],
        compiler_params=pltpu.CompilerParams(
            dimension_semantics=("arbitrary", "arbitrary", "arbitrary"), vmem_limit_bytes=VMEM_LIMIT_BYTES),
        name="moba_attn",
    )(q, k, vt, kmean)


def _out_ffn_kernel(x1_ref, y_ref, a_ref, gm_ref, wgate_ref, gluw_ref, glub_ref, wbs_ref, wba_ref, wout_ref,
                    g3_ref, wg_ref, wu_ref, wd_ref, gf_ref, o_ref):
    x1 = x1_ref[...]
    D = x1.shape[1]
    h = _rmsnorm(x1, gm_ref[...]).astype(BF16)
    gates = _dot(h, wgate_ref[...])
    ys = jax.nn.gelu(y_ref[...])
    ys = ys * jax.nn.sigmoid(_dot(ys.astype(BF16), gluw_ref[...]) + glub_ref[...])
    branch_a = _dot(ys.astype(BF16), wbs_ref[...])
    branch_b = _dot(a_ref[...], wba_ref[...])
    merged = jax.nn.sigmoid(gates[:, :D]) * branch_a + jax.nn.sigmoid(gates[:, D:]) * branch_b
    x2 = x1 + _dot(merged.astype(BF16), wout_ref[...])

    h3 = _rmsnorm(x2, g3_ref[...]).astype(BF16)
    act = (jax.nn.silu(_dot(h3, wg_ref[...])) * _dot(h3, wu_ref[...])).astype(BF16)
    x3 = x2 + 0.5 * _dot(act, wd_ref[...])
    o_ref[...] = _rmsnorm(x3, gf_ref[...])


def _out_ffn(x1, y, attn, gm, wgate, gluw, glub, wbs, wba, wout, g3, wg, wu, wd, gf, *, tm):
    T, D = x1.shape
    F = wg.shape[1]
    ssm_w = y.shape[1]
    attn_w = attn.shape[1]
    row = lambda i: (i, 0)
    return pl.pallas_call(
        _out_ffn_kernel,
        grid=(T // tm,),
        in_specs=[
            pl.BlockSpec((tm, D), row), pl.BlockSpec((tm, ssm_w), row), pl.BlockSpec((tm, attn_w), row),
            _const_spec((1, D)), _const_spec((D, 2 * D)), _const_spec((ssm_w, ssm_w)), _const_spec((1, ssm_w)),
            _const_spec((ssm_w, D)), _const_spec((attn_w, D)), _const_spec((D, D)),
            _const_spec((1, D)), _const_spec((D, F)), _const_spec((D, F)), _const_spec((F, D)),
            _const_spec((1, D)),
        ],
        out_specs=pl.BlockSpec((tm, D), row),
        out_shape=jax.ShapeDtypeStruct((T, D), F32),
        compiler_params=pltpu.CompilerParams(
            dimension_semantics=("arbitrary",), vmem_limit_bytes=VMEM_LIMIT_BYTES),
        name="out_ffn2",
    )(x1, y, attn, gm, wgate, gluw, glub, wbs, wba, wout, g3, wg, wu, wd, gf)


def _rope_tables(seq):
    pos = jnp.arange(seq, dtype=F32)
    inv_freq = ROPE_THETA ** (-jnp.arange(0, HEAD_DIM, 2, dtype=F32) / HEAD_DIM)
    ang = pos[:, None] * inv_freq[None, :]
    cos = jnp.cos(ang)
    sin = jnp.sin(ang)
    cos_t = jnp.tile(cos, (1, 2 * LANES // HEAD_DIM))
    sin_t = jnp.tile(jnp.concatenate([-sin, sin], axis=1), (1, LANES // HEAD_DIM))
    return cos_t, sin_t


TM_FFN = 256
TM_S5 = 2048


def kernel(x, ffn1_norm, ffn1_w_gate, ffn1_w_up, ffn1_w_down, mix_norm, w_in, ssm_a_re, ssm_a_im, ssm_b_re, ssm_b_im, ssm_c_re, ssm_c_im, ssm_d, ssm_log_dt, glu_w, glu_b, w_branch_ssm, w_branch_attn, w_out, ffn2_norm, ffn2_w_gate, ffn2_w_up, ffn2_w_down, final_norm):
    B, L, D = x.shape
    ssm_w = glu_w.shape[1]
    attn_w = w_branch_attn.shape[1]
    T = B * L
    assert ffn1_norm.shape[0] == 1, "single-layer trunk only"
    assert attn_w == N_HEADS * HEAD_DIM and L % TM_S5 == 0 and T % TM_FFN == 0
    assert TM_FFN % MOBA_BLOCK == 0 and L % TM_FFN == 0
    bf = lambda w: w[0].astype(BF16)
    row = lambda v: v[0][None]
    cos_t, sin_t = _rope_tables(L)
    nseg = TM_S5 // (SUBLANES * CHUNK)
    wi = w_in[0]
    off_v = ssm_w + 2 * attn_w
    off_g = ssm_w + 3 * attn_w
    x1, u, q, k, vt, kmean8 = _ffn_inproj(
        x.reshape(T, D), row(ffn1_norm), bf(ffn1_w_gate), bf(ffn1_w_up), bf(ffn1_w_down), row(mix_norm),
        wi[:, :off_v].astype(BF16), wi[:, off_v:off_g].T.astype(BF16), cos_t, sin_t, seq=L, tm=TM_FFN)
    w1, wc, a8, pw = _s5_tables(ssm_a_re[0], ssm_a_im[0], ssm_b_re[0], ssm_b_im[0],
                                ssm_c_re[0], ssm_c_im[0], ssm_log_dt[0], nseg)
    y = _s5(u, w1, wc, a8, pw, row(ssm_d), batch=B, seq=L, tm=TM_S5)
    kmean = kmean8[::SUBLANES].reshape(B, L // MOBA_BLOCK, attn_w)
    attn = _moba(q, k, vt, kmean, batch=B, seq=L)
    out = _out_ffn(
        x1, y, attn, row(mix_norm), wi[:, off_g:].astype(BF16), bf(glu_w), row(glu_b), bf(w_branch_ssm),
        bf(w_branch_attn), bf(w_out), row(ffn2_norm), bf(ffn2_w_gate), bf(ffn2_w_up), bf(ffn2_w_down),
        final_norm[None], tm=TM_FFN)
    return out.reshape(B, L, D)
```

```python
import functools
import math

import jax
import jax.numpy as jnp
from jax import lax
from jax.experimental import pallas as pl
from jax.experimental.pallas import tpu as pltpu

F32 = jnp.float32
BF16 = jnp.bfloat16

N_HEADS = 8
HEAD_DIM = 64
MOBA_BLOCK = 256
MOBA_TOPK = 3
SSM_GROUP = 16
SSM_STATE = 64
ROPE_THETA = 10000.0
RMS_EPS = 1e-6
NEG_INF = -1e30

LANES = 128
SUBLANES = 8
VMEM_LIMIT_BYTES = 56 * 1024 * 1024

CHUNK = 8
OCTET = LANES // SSM_GROUP
OCT_STATE = OCTET * SSM_STATE


def _rmsnorm(x, gain):
    inv = lax.rsqrt(jnp.mean(x * x, axis=-1, keepdims=True) + RMS_EPS)
    return (x * inv) * gain


def _dot(a, b):
    return jnp.dot(a, b, preferred_element_type=F32)


def _const_spec(shape):
    nd = len(shape)
    return pl.BlockSpec(shape, lambda *_: (0,) * nd, pipeline_mode=pl.Buffered(1))


def _rotary_tile(x, cos, sin_signed, first_half):
    swapped = jnp.where(first_half, pltpu.roll(x, LANES - HEAD_DIM // 2, 1), pltpu.roll(x, HEAD_DIM // 2, 1))
    return x * cos + swapped * sin_signed


def _ffn_inproj_kernel(x_ref, g1_ref, wg_ref, wu_ref, wd_ref, g2_ref, win_ref, wvt_ref, cos_ref, sin_ref,
                       x1_ref, u_ref, q_ref, k_ref, vt_ref, kmean_ref, *, attn_w, ssm_w, nblocks):
    x = x_ref[...]
    h = _rmsnorm(x, g1_ref[...]).astype(BF16)
    act = (jax.nn.silu(_dot(h, wg_ref[...])) * _dot(h, wu_ref[...])).astype(BF16)
    x1 = x + 0.5 * _dot(act, wd_ref[...])
    x1_ref[...] = x1

    h2 = _rmsnorm(x1, g2_ref[...]).astype(BF16)
    proj = _dot(h2, win_ref[...])
    u_ref[...] = proj[:, :ssm_w]

    cos = cos_ref[...]
    sin_signed = sin_ref[...]
    lane = lax.broadcasted_iota(jnp.int32, cos.shape, 1)
    first_half = (lane % HEAD_DIM) < (HEAD_DIM // 2)
    scale = HEAD_DIM ** -0.5
    tm = x.shape[0]
    nblk = tm // MOBA_BLOCK
    key_blk = (pl.program_id(0) * nblk + lax.broadcasted_iota(jnp.int32, cos.shape, 0) // MOBA_BLOCK) % nblocks
    blk_onehot = (lane == key_blk).astype(BF16)
    for t in range(attn_w // LANES):
        qs = proj[:, ssm_w + t * LANES: ssm_w + (t + 1) * LANES]
        ks = proj[:, ssm_w + attn_w + t * LANES: ssm_w + attn_w + (t + 1) * LANES]
        q_ref[:, t * LANES:(t + 1) * LANES] = (_rotary_tile(qs, cos, sin_signed, first_half) * scale).astype(BF16)
        kr = _rotary_tile(ks, cos, sin_signed, first_half)
        k_ref[:, 2 * t * LANES:(2 * t + 1) * LANES] = kr.astype(BF16)
        k_ref[:, (2 * t + 1) * LANES:(2 * t + 2) * LANES] = blk_onehot
        for j in range(nblk):
            mean = jnp.mean(kr[j * MOBA_BLOCK:(j + 1) * MOBA_BLOCK], axis=0, keepdims=True)
            kmean_ref[j * SUBLANES:(j + 1) * SUBLANES, t * LANES:(t + 1) * LANES] = jnp.broadcast_to(
                mean, (SUBLANES, LANES))

    vt = lax.dot_general(wvt_ref[...], h2, (((1,), (1,)), ((), ())), preferred_element_type=F32).astype(BF16)
    ones = jnp.ones((HEAD_DIM, MOBA_BLOCK), BF16)
    for j in range(nblk):
        for hd in range(attn_w // HEAD_DIM):
            vt_ref[j, 2 * hd * HEAD_DIM:(2 * hd + 1) * HEAD_DIM, :] = (
                vt[hd * HEAD_DIM:(hd + 1) * HEAD_DIM, j * MOBA_BLOCK:(j + 1) * MOBA_BLOCK])
            vt_ref[j, (2 * hd + 1) * HEAD_DIM:(2 * hd + 2) * HEAD_DIM, :] = ones


def _ffn_inproj(x2d, g1, wg, wu, wd, g2, win_uqk, wv_t, cos_t, sin_t, *, seq, tm):
    T, D = x2d.shape
    F = wg.shape[1]
    attn_w = wv_t.shape[0]
    ssm_w = win_uqk.shape[1] - 2 * attn_w
    nblk = tm // MOBA_BLOCK
    tiles_per_seq = seq // tm
    row = lambda i: (i, 0)
    return pl.pallas_call(
        functools.partial(_ffn_inproj_kernel, attn_w=attn_w, ssm_w=ssm_w, nblocks=seq // MOBA_BLOCK),
        grid=(T // tm,),
        in_specs=[
            pl.BlockSpec((tm, D), row),
            _const_spec((1, D)), _const_spec((D, F)), _const_spec((D, F)), _const_spec((F, D)),
            _const_spec((1, D)), _const_spec((D, ssm_w + 2 * attn_w)), _const_spec((attn_w, D)),
            pl.BlockSpec((tm, LANES), lambda i: (i % tiles_per_seq, 0)),
            pl.BlockSpec((tm, LANES), lambda i: (i % tiles_per_seq, 0)),
        ],
        out_specs=[
            pl.BlockSpec((tm, D), row),
            pl.BlockSpec((tm, ssm_w), row),
            pl.BlockSpec((tm, attn_w), row),
            pl.BlockSpec((tm, 2 * attn_w), row),
            pl.BlockSpec((nblk, 2 * attn_w, MOBA_BLOCK), lambda i: (i, 0, 0)),
            pl.BlockSpec((nblk * SUBLANES, attn_w), row),
        ],
        out_shape=[
            jax.ShapeDtypeStruct((T, D), F32),
            jax.ShapeDtypeStruct((T, ssm_w), F32),
            jax.ShapeDtypeStruct((T, attn_w), BF16),
            jax.ShapeDtypeStruct((T, 2 * attn_w), BF16),
            jax.ShapeDtypeStruct((T // MOBA_BLOCK, 2 * attn_w, MOBA_BLOCK), BF16),
            jax.ShapeDtypeStruct((T // MOBA_BLOCK * SUBLANES, attn_w), F32),
        ],
        compiler_params=pltpu.CompilerParams(
            dimension_semantics=("arbitrary",), vmem_limit_bytes=VMEM_LIMIT_BYTES),
        name="ffn1_inproj",
    )(x2d, g1, wg, wu, wd, g2, win_uqk, wv_t, cos_t, sin_t)


def _s5_kernel(u_ref, w1_ref, wc_ref, a8_ref, pw_ref, d_ref, y_ref,
               lhs_scr, r_scr, e_scr, ein_scr, carry_scr, *, tm, nseg):
    S = OCT_STATE
    CW = CHUNK * LANES
    seg_stride = tm // SUBLANES

    @pl.when(pl.program_id(2) == 0)
    def _():
        carry_scr[...] = jnp.zeros_like(carry_scr)

    for j in range(nseg):
        for i in range(CHUNK):
            lhs_scr[j * SUBLANES:(j + 1) * SUBLANES, i * LANES:(i + 1) * LANES] = (
                u_ref[pl.ds(CHUNK * j + i, SUBLANES, stride=seg_stride), :])

    r_scr[...] = _dot(lhs_scr[...].astype(BF16), w1_ref[0])

    a8 = a8_ref[0]
    ar = jnp.broadcast_to(a8[:, :S], (SUBLANES, S))
    ai = jnp.broadcast_to(a8[:, S:], (SUBLANES, S))
    er = jnp.zeros((SUBLANES, S), F32)
    ei = jnp.zeros((SUBLANES, S), F32)
    for j in range(nseg):
        rows = slice(j * SUBLANES, (j + 1) * SUBLANES)
        e_scr[rows, :S] = er
        e_scr[rows, S:] = ei
        zr = r_scr[rows, CW:CW + S]
        zi = r_scr[rows, CW + S:]
        er, ei = ar * er - ai * ei + zr, ar * ei + ai * er + zi
    ein_scr[:, :S] = er
    ein_scr[:, S:] = ei

    pw = pw_ref[0]
    pnr = pw[nseg:nseg + 1, :S]
    pni = pw[nseg:nseg + 1, S:]
    cr = carry_scr[0:1, :S]
    ci = carry_scr[0:1, S:]
    for s in range(SUBLANES):
        fr = ein_scr[s:s + 1, :S]
        fi = ein_scr[s:s + 1, S:]
        ein_scr[s:s + 1, :S] = cr
        ein_scr[s:s + 1, S:] = ci
        cr, ci = fr + pnr * cr - pni * ci, fi + pnr * ci + pni * cr
    carry_scr[0:1, :S] = cr
    carry_scr[0:1, S:] = ci

    einr = ein_scr[:, :S]
    eini = ein_scr[:, S:]
    for j in range(nseg):
        rows = slice(j * SUBLANES, (j + 1) * SUBLANES)
        pr = pw[j:j + 1, :S]
        pi = pw[j:j + 1, S:]
        e_scr[rows, :S] = e_scr[rows, :S] + (pr * einr - pi * eini)
        e_scr[rows, S:] = e_scr[rows, S:] + (pr * eini + pi * einr)

    r_scr[:, :CW] = r_scr[:, :CW] + _dot(e_scr[...].astype(BF16), wc_ref[0])

    d = d_ref[...]
    for j in range(nseg):
        rows = slice(j * SUBLANES, (j + 1) * SUBLANES)
        for i in range(CHUNK):
            cols = slice(i * LANES, (i + 1) * LANES)
            y_ref[pl.ds(CHUNK * j + i, SUBLANES, stride=seg_stride), :] = (
                r_scr[rows, cols] + d * lhs_scr[rows, cols])


def _s5(u, w1, wc, a8, pw, d, *, batch, seq, tm):
    T, ssm_w = u.shape
    nseg = tm // (SUBLANES * CHUNK)
    nc = tm // CHUNK
    CW = CHUNK * LANES
    S2 = 2 * OCT_STATE
    n_oct = ssm_w // LANES
    tiles = seq // tm
    return pl.pallas_call(
        functools.partial(_s5_kernel, tm=tm, nseg=nseg),
        grid=(n_oct, batch, tiles),
        in_specs=[
            pl.BlockSpec((tm, LANES), lambda o, b, t: (b * tiles + t, o)),
            pl.BlockSpec((1, CW, CW + S2), lambda o, b, t: (o, 0, 0)),
            pl.BlockSpec((1, S2, CW), lambda o, b, t: (o, 0, 0)),
            pl.BlockSpec((1, 1, S2), lambda o, b, t: (o, 0, 0)),
            pl.BlockSpec((1, nseg + 1, S2), lambda o, b, t: (o, 0, 0)),
            pl.BlockSpec((1, LANES), lambda o, b, t: (0, o)),
        ],
        out_specs=pl.BlockSpec((tm, LANES), lambda o, b, t: (b * tiles + t, o)),
        out_shape=jax.ShapeDtypeStruct((T, ssm_w), F32),
        scratch_shapes=[
            pltpu.VMEM((nc, CW), F32),
            pltpu.VMEM((nc, CW + S2), F32),
            pltpu.VMEM((nc, S2), F32),
            pltpu.VMEM((SUBLANES, S2), F32),
            pltpu.VMEM((SUBLANES, S2), F32),
        ],
        compiler_params=pltpu.CompilerParams(
            dimension_semantics=("arbitrary", "arbitrary", "arbitrary"), vmem_limit_bytes=VMEM_LIMIT_BYTES),
        name="s5_scan",
    )(u, w1, wc, a8, pw, d)


def _s5_tables(a_re, a_im, b_re, b_im, c_re, c_im, log_dt, nseg):
    G, P = a_re.shape
    H = b_re.shape[-1]
    n_oct = G // OCTET
    dt = jnp.exp(log_dt)[:, None]
    lam_r = a_re * dt
    lam_i = a_im * dt

    def powers(n):
        n = n.astype(F32)[:, None, None]
        mag = jnp.exp(lam_r * n)
        return mag * jnp.cos(lam_i * n), mag * jnp.sin(lam_i * n)

    pr, pi = powers(jnp.arange(CHUNK + 1))
    den = a_re * a_re + a_im * a_im
    nr = pr[1] - 1.0
    ni = pi[1]
    fr = ((nr * a_re + ni * a_im) / den)[..., None]
    fi = ((ni * a_re - nr * a_im) / den)[..., None]
    bbr = fr * b_re - fi * b_im
    bbi = fr * b_im + fi * b_re

    def octet_rows(x):
        lead = x.shape[:-3]
        x = x.reshape(lead + (n_oct, OCTET * H, P))
        x = jnp.concatenate([x, x], axis=-1)
        return jnp.moveaxis(x, len(lead), 0)

    bt = octet_rows(jnp.stack([bbr, bbi]).transpose(0, 1, 3, 2))
    ct = octet_rows(jnp.stack([c_re, c_im]))
    pwr = jnp.broadcast_to(jnp.stack([pr, pi], axis=1)[:, :, :, None, :], (CHUNK + 1, 2, G, H, P))
    pwr = octet_rows(pwr).reshape(n_oct, 2 * (CHUNK + 1), LANES, LANES)
    w1, w_c = _s5_table_call(pwr, bt, ct)

    a8 = jnp.concatenate([pr[CHUNK].reshape(n_oct, 1, OCT_STATE), pi[CHUNK].reshape(n_oct, 1, OCT_STATE)], axis=2)
    qr, qi = powers(CHUNK * jnp.arange(nseg + 1))
    pw = jnp.concatenate([qr.reshape(nseg + 1, n_oct, OCT_STATE), qi.reshape(nseg + 1, n_oct, OCT_STATE)], axis=2)
    return w1, w_c, a8, pw.transpose(1, 0, 2)


def _s5_table_kernel(pw_ref, bt_ref, ct_ref, w1_ref, wc_ref):
    S = OCT_STATE
    CW = CHUNK * LANES
    row = lax.broadcasted_iota(jnp.int32, (LANES, LANES), 0)
    lane = lax.broadcasted_iota(jnp.int32, (LANES, LANES), 1)
    same_group = (row // SSM_GROUP) == (lane // SSM_GROUP)
    first_copy = lane < SSM_STATE
    wide_row = lax.broadcasted_iota(jnp.int32, (LANES, S), 0)
    wide_lane = lax.broadcasted_iota(jnp.int32, (LANES, S), 1)
    own_states = (wide_row // SSM_GROUP) == (wide_lane // SSM_STATE)

    def spread(x):
        return jnp.where(own_states, jnp.concatenate([x] * (S // LANES), axis=1), 0.0)

    def nt_dot(a, b):
        return lax.dot_general(a, b, (((1,), (1,)), ((), ())), precision=lax.Precision.HIGHEST,
                               preferred_element_type=F32)

    btr, bti = bt_ref[0, 0], bt_ref[0, 1]
    ctr, cti = ct_ref[0, 0], ct_ref[0, 1]
    ctr_once = jnp.where(first_copy, ctr, 0.0)
    cti_once = jnp.where(first_copy, cti, 0.0)
    zero_tile = jnp.zeros((LANES, LANES), BF16)
    for n in range(CHUNK):
        pr, pi = pw_ref[0, 2 * n], pw_ref[0, 2 * n + 1]
        xr = btr * pr - bti * pi
        xi = btr * pi + bti * pr
        kern = jnp.where(same_group, nt_dot(xr, ctr_once) - nt_dot(xi, cti_once), 0.0).astype(BF16)
        for i in range(CHUNK - n):
            w1_ref[0, i * LANES:(i + 1) * LANES, (i + n) * LANES:(i + n + 1) * LANES] = kern
        for i in range(n, CHUNK) if n else ():
            w1_ref[0, i * LANES:(i + 1) * LANES, (i - n) * LANES:(i - n + 1) * LANES] = zero_tile
        i = CHUNK - 1 - n
        w1_ref[0, i * LANES:(i + 1) * LANES, CW:CW + S] = spread(xr).astype(BF16)
        w1_ref[0, i * LANES:(i + 1) * LANES, CW + S:] = spread(xi).astype(BF16)
    for j in range(CHUNK):
        pr, pi = pw_ref[0, 2 * (j + 1)], pw_ref[0, 2 * (j + 1) + 1]
        wc_ref[0, :S, j * LANES:(j + 1) * LANES] = spread(ctr * pr - cti * pi).T.astype(BF16)
        wc_ref[0, S:, j * LANES:(j + 1) * LANES] = spread(-(ctr * pi + cti * pr)).T.astype(BF16)


def _s5_table_call(pwr, bt, ct):
    n_oct = pwr.shape[0]
    CW = CHUNK * LANES
    S2 = 2 * OCT_STATE
    tile4 = lambda n: pl.BlockSpec((1, n, LANES, LANES), lambda o: (o, 0, 0, 0))
    return pl.pallas_call(
        _s5_table_kernel,
        grid=(n_oct,),
        in_specs=[tile4(pwr.shape[1]), tile4(2), tile4(2)],
        out_specs=[pl.BlockSpec((1, CW, CW + S2), lambda o: (o, 0, 0)),
                   pl.BlockSpec((1, S2, CW), lambda o: (o, 0, 0))],
        out_shape=[jax.ShapeDtypeStruct((n_oct, CW, CW + S2), BF16),
                   jax.ShapeDtypeStruct((n_oct, S2, CW), BF16)],
        compiler_params=pltpu.CompilerParams(
            dimension_semantics=("arbitrary",), vmem_limit_bytes=VMEM_LIMIT_BYTES),
        name="s5_tables",
    )(pwr, bt, ct)


KV_BLOCKS_PER_STEP = 2


def _moba_kernel(q_ref, k_ref, vt_ref, km_ref, o_ref, qt_scr, s0_scr, s1_scr, m_scr, acc_scr, *, nblocks):
    t = pl.program_id(2)
    BK = MOBA_BLOCK
    NB = KV_BLOCKS_PER_STEP
    QT = q_ref.shape[0]
    q2t = q_ref[...].astype(F32).T
    feat = lax.broadcasted_iota(jnp.int32, q2t.shape, 0)
    km = km_ref[0]
    km_hi = km.astype(BF16)
    km_lo = (km - km_hi.astype(F32)).astype(BF16)
    blk = lax.broadcasted_iota(jnp.int32, (nblocks, QT), 0)
    blk_f = blk.astype(F32)
    own = t * (QT // BK) + lax.broadcasted_iota(jnp.int32, (nblocks, QT), 1) // BK
    pad = jnp.zeros((LANES - nblocks, QT), BF16)

    for a in range(2):
        qat = jnp.where((feat >= a * HEAD_DIM) & (feat < (a + 1) * HEAD_DIM), q2t, 0.0).astype(BF16)
        gate = _dot(km_hi, qat) + _dot(km_lo, qat)
        gate = jnp.where(blk < own, gate, NEG_INF)
        sel = jnp.zeros(gate.shape, jnp.bool_)
        for _ in range(MOBA_TOPK):
            top = jnp.max(gate, axis=0, keepdims=True)
            idx = jnp.min(jnp.where(gate == top, blk_f, float(nblocks)), axis=0, keepdims=True)
            pick = blk_f == idx
            sel = sel | pick
            gate = jnp.where(pick, -jnp.inf, gate)
        bias = jnp.where((sel & (blk < own)) | (blk == own), 0.0, NEG_INF).astype(BF16)
        qt_scr[a] = jnp.concatenate([qat, bias, pad], axis=0)
        m_scr[a] = jnp.full((1, QT), NEG_INF, F32)
        acc_scr[a] = jnp.zeros((LANES, QT), F32)

    def scores(c, s_scr):
        kb = k_ref[pl.ds(pl.multiple_of(c * (NB * BK), NB * BK), NB * BK), :]
        for a in range(2):
            s_scr[a] = _dot(kb, qt_scr[a])

    def consume(c, s_scr):
        for a in range(2):
            vta = jnp.concatenate([vt_ref[c * NB + j, a * LANES:(a + 1) * LANES, :] for j in range(NB)], axis=1)
            s = s_scr[a]
            m_old = m_scr[a]
            m_new = jnp.maximum(m_old, jnp.max(s, axis=0, keepdims=True))
            p = jnp.exp(s - m_new).astype(BF16)
            acc_scr[a] = jnp.exp(m_old - m_new) * acc_scr[a] + _dot(vta, p)
            m_scr[a] = m_new

    tri = (lax.broadcasted_iota(jnp.int32, (BK, BK), 0) <= lax.broadcasted_iota(jnp.int32, (BK, BK), 1))

    def causal_patch(e, s_scr):
        for a in range(2):
            for j in range(NB):
                rows = slice(j * BK, (j + 1) * BK)
                cols = slice((e * NB + j) * BK, (e * NB + j + 1) * BK)
                s_scr[a, rows, cols] = jnp.where(tri, s_scr[a, rows, cols], NEG_INF)

    scores(0, s0_scr)

    def body(i, carry):
        scores(2 * i + 1, s1_scr)
        consume(2 * i, s0_scr)
        scores(2 * i + 2, s0_scr)
        consume(2 * i + 1, s1_scr)
        return carry

    lax.fori_loop(0, t, body, 0)
    scores(2 * t + 1, s1_scr)
    causal_patch(0, s0_scr)
    consume(2 * t, s0_scr)
    causal_patch(1, s1_scr)
    consume(2 * t + 1, s1_scr)
    ot = jnp.concatenate([acc_scr[a, :HEAD_DIM] / acc_scr[a, HEAD_DIM:HEAD_DIM + 1] for a in range(2)], axis=0)
    o_ref[...] = ot.T.astype(BF16)


def _moba(q, k, vt, kmean, *, batch, seq):
    T, attn_w = q.shape
    nblocks = seq // MOBA_BLOCK
    n_pairs = attn_w // LANES
    BK = MOBA_BLOCK
    QT = 2 * KV_BLOCKS_PER_STEP * BK
    tiles = seq // QT
    return pl.pallas_call(
        functools.partial(_moba_kernel, nblocks=nblocks),
        grid=(batch, n_pairs, tiles),
        in_specs=[
            pl.BlockSpec((QT, LANES), lambda b, h, i: (b * tiles + i, h)),
            pl.BlockSpec((seq, 2 * LANES), lambda b, h, i: (b, h)),
            pl.BlockSpec((nblocks, 2 * LANES, BK), lambda b, h, i: (b, h, 0)),
            pl.BlockSpec((1, nblocks, LANES), lambda b, h, i: (b, 0, h)),
        ],
        out_specs=pl.BlockSpec((QT, LANES), lambda b, h, i: (b * tiles + i, h)),
        out_shape=jax.ShapeDtypeStruct((T, attn_w), BF16),
        scratch_shapes=[
            pltpu.VMEM((2, 2 * LANES, QT), BF16),
            pltpu.VMEM((2, KV_BLOCKS_PER_STEP * BK, QT), F32),
            pltpu.VMEM((2, KV_BLOCKS_PER_STEP * BK, QT), F32),
            pltpu.VMEM((2, 1, QT), F32),
            pltpu.VMEM((2, LANES, QT), F32),
        ],
        compiler_params=pltpu.CompilerParams(
            dimension_semantics=("arbitrary", "arbitrary", "arbitrary"), vmem_limit_bytes=VMEM_LIMIT_BYTES),
        name="moba_attn",
    )(q, k, vt, kmean)


def _out_ffn_kernel(x1_ref, y_ref, a_ref, gm_ref, wgate_ref, gluw_ref, glub_ref, wbs_ref, wba_ref, wout_ref,
                    g3_ref, wg_ref, wu_ref, wd_ref, gf_ref, o_ref):
    x1 = x1_ref[...]
    D = x1.shape[1]
    h = _rmsnorm(x1, gm_ref[...]).astype(BF16)
    gates = _dot(h, wgate_ref[...])
    ys = jax.nn.gelu(y_ref[...])
    ys = ys * jax.nn.sigmoid(_dot(ys.astype(BF16), gluw_ref[...]) + glub_ref[...])
    branch_a = _dot(ys.astype(BF16), wbs_ref[...])
    branch_b = _dot(a_ref[...], wba_ref[...])
    merged = jax.nn.sigmoid(gates[:, :D]) * branch_a + jax.nn.sigmoid(gates[:, D:]) * branch_b
    x2 = x1 + _dot(merged.astype(BF16), wout_ref[...])

    h3 = _rmsnorm(x2, g3_ref[...]).astype(BF16)
    act = (jax.nn.silu(_dot(h3, wg_ref[...])) * _dot(h3, wu_ref[...])).astype(BF16)
    x3 = x2 + 0.5 * _dot(act, wd_ref[...])
    o_ref[...] = _rmsnorm(x3, gf_ref[...])


def _out_ffn(x1, y, attn, gm, wgate, gluw, glub, wbs, wba, wout, g3, wg, wu, wd, gf, *, tm):
    T, D = x1.shape
    F = wg.shape[1]
    ssm_w = y.shape[1]
    attn_w = attn.shape[1]
    row = lambda i: (i, 0)
    return pl.pallas_call(
        _out_ffn_kernel,
        grid=(T // tm,),
        in_specs=[
            pl.BlockSpec((tm, D), row), pl.BlockSpec((tm, ssm_w), row), pl.BlockSpec((tm, attn_w), row),
            _const_spec((1, D)), _const_spec((D, 2 * D)), _const_spec((ssm_w, ssm_w)), _const_spec((1, ssm_w)),
            _const_spec((ssm_w, D)), _const_spec((attn_w, D)), _const_spec((D, D)),
            _const_spec((1, D)), _const_spec((D, F)), _const_spec((D, F)), _const_spec((F, D)),
            _const_spec((1, D)),
        ],
        out_specs=pl.BlockSpec((tm, D), row),
        out_shape=jax.ShapeDtypeStruct((T, D), F32),
        compiler_params=pltpu.CompilerParams(
            dimension_semantics=("arbitrary",), vmem_limit_bytes=VMEM_LIMIT_BYTES),
        name="out_ffn2",
    )(x1, y, attn, gm, wgate, gluw, glub, wbs, wba, wout, g3, wg, wu, wd, gf)


def _rope_tables(seq):
    pos = jnp.arange(seq, dtype=F32)
    inv_freq = ROPE_THETA ** (-jnp.arange(0, HEAD_DIM, 2, dtype=F32) / HEAD_DIM)
    ang = pos[:, None] * inv_freq[None, :]
    cos = jnp.cos(ang)
    sin = jnp.sin(ang)
    cos_t = jnp.tile(cos, (1, 2 * LANES // HEAD_DIM))
    sin_t = jnp.tile(jnp.concatenate([-sin, sin], axis=1), (1, LANES // HEAD_DIM))
    return cos_t, sin_t


TM_FFN = 256
TM_S5 = 2048


def kernel(x, ffn1_norm, ffn1_w_gate, ffn1_w_up, ffn1_w_down, mix_norm, w_in, ssm_a_re, ssm_a_im, ssm_b_re, ssm_b_im, ssm_c_re, ssm_c_im, ssm_d, ssm_log_dt, glu_w, glu_b, w_branch_ssm, w_branch_attn, w_out, ffn2_norm, ffn2_w_gate, ffn2_w_up, ffn2_w_down, final_norm):
    B, L, D = x.shape
    ssm_w = glu_w.shape[1]
    attn_w = w_branch_attn.shape[1]
    T = B * L
    assert ffn1_norm.shape[0] == 1, "single-layer trunk only"
    assert attn_w == N_HEADS * HEAD_DIM and L % TM_S5 == 0 and T % TM_FFN == 0
    assert TM_FFN % MOBA_BLOCK == 0 and L % TM_FFN == 0
    bf = lambda w: w[0].astype(BF16)
    row = lambda v: v[0][None]
    cos_t, sin_t = _rope_tables(L)
    nseg = TM_S5 // (SUBLANES * CHUNK)
    wi = w_in[0]
    off_v = ssm_w + 2 * attn_w
    off_g = ssm_w + 3 * attn_w
    x1, u, q, k, vt, kmean8 = _ffn_inproj(
        x.reshape(T, D), row(ffn1_norm), bf(ffn1_w_gate), bf(ffn1_w_up), bf(ffn1_w_down), row(mix_norm),
        wi[:, :off_v].astype(BF16), wi[:, off_v:off_g].T.astype(BF16), cos_t, sin_t, seq=L, tm=TM_FFN)
    w1, wc, a8, pw = _s5_tables(ssm_a_re[0], ssm_a_im[0], ssm_b_re[0], ssm_b_im[0],
                                ssm_c_re[0], ssm_c_im[0], ssm_log_dt[0], nseg)
    y = _s5(u, w1, wc, a8, pw, row(ssm_d), batch=B, seq=L, tm=TM_S5)
    kmean = kmean8[::SUBLANES].reshape(B, L // MOBA_BLOCK, attn_w)
    attn = _moba(q, k, vt, kmean, batch=B, seq=L)
    out = _out_ffn(
        x1, y, attn, row(mix_norm), wi[:, off_g:].astype(BF16), bf(glu_w), row(glu_b), bf(w_branch_ssm),
        bf(w_branch_attn), bf(w_out), row(ffn2_norm), bf(ffn2_w_gate), bf(ffn2_w_up), bf(ffn2_w_down),
        final_norm[None], tm=TM_FFN)
    return out.reshape(B, L, D)
```

```python
import functools
import math

import jax
import jax.numpy as jnp
from jax import lax
from jax.experimental import pallas as pl
from jax.experimental.pallas import tpu as pltpu

F32 = jnp.float32
BF16 = jnp.bfloat16

N_HEADS = 8
HEAD_DIM = 64
MOBA_BLOCK = 256
MOBA_TOPK = 3
SSM_GROUP = 16
SSM_STATE = 64
ROPE_THETA = 10000.0
RMS_EPS = 1e-6
NEG_INF = -1e30

LANES = 128
SUBLANES = 8
VMEM_LIMIT_BYTES = 56 * 1024 * 1024

CHUNK = 8
OCTET = LANES // SSM_GROUP
OCT_STATE = OCTET * SSM_STATE

VT_ROWS = HEAD_DIM + 16


def _rmsnorm(x, gain):
    inv = lax.rsqrt(jnp.mean(x * x, axis=-1, keepdims=True) + RMS_EPS)
    return (x * inv) * gain


def _dot(a, b):
    return jnp.dot(a, b, preferred_element_type=F32)


def _const_spec(shape):
    nd = len(shape)
    return pl.BlockSpec(shape, lambda *_: (0,) * nd, pipeline_mode=pl.Buffered(1))


def _rotary_tile(x, cos, sin_signed, first_half):
    swapped = jnp.where(first_half, pltpu.roll(x, LANES - HEAD_DIM // 2, 1), pltpu.roll(x, HEAD_DIM // 2, 1))
    return x * cos + swapped * sin_signed


def _ffn_inproj_kernel(x_ref, g1_ref, wg_ref, wu_ref, wd_ref, g2_ref, win_ref, wvt_ref, cos_ref, sin_ref,
                       x1_ref, u_ref, q_ref, k_ref, vt_ref, kmean_ref, *, attn_w, ssm_w, nblocks):
    x = x_ref[...]
    h = _rmsnorm(x, g1_ref[...]).astype(BF16)
    act = (jax.nn.silu(_dot(h, wg_ref[...])) * _dot(h, wu_ref[...])).astype(BF16)
    x1 = x + 0.5 * _dot(act, wd_ref[...])
    x1_ref[...] = x1

    h2 = _rmsnorm(x1, g2_ref[...]).astype(BF16)
    proj = _dot(h2, win_ref[...])
    u_ref[...] = proj[:, :ssm_w]

    cos = cos_ref[...]
    sin_signed = sin_ref[...]
    lane = lax.broadcasted_iota(jnp.int32, cos.shape, 1)
    first_half = (lane % HEAD_DIM) < (HEAD_DIM // 2)
    scale = HEAD_DIM ** -0.5 * math.log2(math.e)
    tm = x.shape[0]
    nblk = tm // MOBA_BLOCK
    key_blk = (pl.program_id(0) * nblk + lax.broadcasted_iota(jnp.int32, cos.shape, 0) // MOBA_BLOCK) % nblocks
    blk_onehot = (lane == key_blk).astype(BF16)
    for t in range(attn_w // LANES):
        qs = proj[:, ssm_w + t * LANES: ssm_w + (t + 1) * LANES]
        ks = proj[:, ssm_w + attn_w + t * LANES: ssm_w + attn_w + (t + 1) * LANES]
        q_ref[:, t * LANES:(t + 1) * LANES] = (_rotary_tile(qs, cos, sin_signed, first_half) * scale).astype(BF16)
        kr = _rotary_tile(ks, cos, sin_signed, first_half)
        k_ref[:, 2 * t * LANES:(2 * t + 1) * LANES] = kr.astype(BF16)
        k_ref[:, (2 * t + 1) * LANES:(2 * t + 2) * LANES] = blk_onehot
        for j in range(nblk):
            mean = jnp.mean(kr[j * MOBA_BLOCK:(j + 1) * MOBA_BLOCK], axis=0, keepdims=True)
            kmean_ref[j * SUBLANES:(j + 1) * SUBLANES, t * LANES:(t + 1) * LANES] = jnp.broadcast_to(
                mean, (SUBLANES, LANES))

    vt = lax.dot_general(wvt_ref[...], h2, (((1,), (1,)), ((), ())), preferred_element_type=F32).astype(BF16)
    ones = jnp.ones((VT_ROWS - HEAD_DIM, MOBA_BLOCK), BF16)
    for j in range(nblk):
        for hd in range(attn_w // HEAD_DIM):
            vt_ref[j, hd * VT_ROWS:hd * VT_ROWS + HEAD_DIM, :] = (
                vt[hd * HEAD_DIM:(hd + 1) * HEAD_DIM, j * MOBA_BLOCK:(j + 1) * MOBA_BLOCK])
            vt_ref[j, hd * VT_ROWS + HEAD_DIM:(hd + 1) * VT_ROWS, :] = ones


def _ffn_inproj(x2d, g1, wg, wu, wd, g2, win_uqk, wv_t, cos_t, sin_t, *, seq, tm):
    T, D = x2d.shape
    F = wg.shape[1]
    attn_w = wv_t.shape[0]
    ssm_w = win_uqk.shape[1] - 2 * attn_w
    nblk = tm // MOBA_BLOCK
    tiles_per_seq = seq // tm
    row = lambda i: (i, 0)
    return pl.pallas_call(
        functools.partial(_ffn_inproj_kernel, attn_w=attn_w, ssm_w=ssm_w, nblocks=seq // MOBA_BLOCK),
        grid=(T // tm,),
        in_specs=[
            pl.BlockSpec((tm, D), row),
            _const_spec((1, D)), _const_spec((D, F)), _const_spec((D, F)), _const_spec((F, D)),
            _const_spec((1, D)), _const_spec((D, ssm_w + 2 * attn_w)), _const_spec((attn_w, D)),
            pl.BlockSpec((tm, LANES), lambda i: (i % tiles_per_seq, 0)),
            pl.BlockSpec((tm, LANES), lambda i: (i % tiles_per_seq, 0)),
        ],
        out_specs=[
            pl.BlockSpec((tm, D), row),
            pl.BlockSpec((tm, ssm_w), row),
            pl.BlockSpec((tm, attn_w), row),
            pl.BlockSpec((tm, 2 * attn_w), row),
            pl.BlockSpec((nblk, attn_w // HEAD_DIM * VT_ROWS, MOBA_BLOCK), lambda i: (i, 0, 0)),
            pl.BlockSpec((nblk * SUBLANES, attn_w), row),
        ],
        out_shape=[
            jax.ShapeDtypeStruct((T, D), F32),
            jax.ShapeDtypeStruct((T, ssm_w), F32),
            jax.ShapeDtypeStruct((T, attn_w), BF16),
            jax.ShapeDtypeStruct((T, 2 * attn_w), BF16),
            jax.ShapeDtypeStruct((T // MOBA_BLOCK, attn_w // HEAD_DIM * VT_ROWS, MOBA_BLOCK), BF16),
            jax.ShapeDtypeStruct((T // MOBA_BLOCK * SUBLANES, attn_w), F32),
        ],
        compiler_params=pltpu.CompilerParams(
            dimension_semantics=("arbitrary",), vmem_limit_bytes=VMEM_LIMIT_BYTES),
        name="ffn1_inproj",
    )(x2d, g1, wg, wu, wd, g2, win_uqk, wv_t, cos_t, sin_t)


def _s5_kernel(u_ref, w1_ref, wc_ref, a8_ref, pw_ref, d_ref, y_ref,
               lhs_scr, r_scr, e_scr, ein_scr, carry_scr, *, tm, nseg):
    S = OCT_STATE
    CW = CHUNK * LANES
    seg_stride = tm // SUBLANES

    @pl.when(pl.program_id(2) == 0)
    def _():
        carry_scr[...] = jnp.zeros_like(carry_scr)

    for j in range(nseg):
        for i in range(CHUNK):
            lhs_scr[j * SUBLANES:(j + 1) * SUBLANES, i * LANES:(i + 1) * LANES] = (
                u_ref[pl.ds(CHUNK * j + i, SUBLANES, stride=seg_stride), :])

    r_scr[...] = _dot(lhs_scr[...].astype(BF16), w1_ref[0])

    a8 = a8_ref[0]
    ar = jnp.broadcast_to(a8[:, :S], (SUBLANES, S))
    ai = jnp.broadcast_to(a8[:, S:], (SUBLANES, S))
    er = jnp.zeros((SUBLANES, S), F32)
    ei = jnp.zeros((SUBLANES, S), F32)
    for j in range(nseg):
        rows = slice(j * SUBLANES, (j + 1) * SUBLANES)
        e_scr[rows, :S] = er
        e_scr[rows, S:] = ei
        zr = r_scr[rows, CW:CW + S]
        zi = r_scr[rows, CW + S:]
        er, ei = ar * er - ai * ei + zr, ar * ei + ai * er + zi
    ein_scr[:, :S] = er
    ein_scr[:, S:] = ei

    pw = pw_ref[0]
    pnr = pw[nseg:nseg + 1, :S]
    pni = pw[nseg:nseg + 1, S:]
    cr = carry_scr[0:1, :S]
    ci = carry_scr[0:1, S:]
    for s in range(SUBLANES):
        fr = ein_scr[s:s + 1, :S]
        fi = ein_scr[s:s + 1, S:]
        ein_scr[s:s + 1, :S] = cr
        ein_scr[s:s + 1, S:] = ci
        cr, ci = fr + pnr * cr - pni * ci, fi + pnr * ci + pni * cr
    carry_scr[0:1, :S] = cr
    carry_scr[0:1, S:] = ci

    einr = ein_scr[:, :S]
    eini = ein_scr[:, S:]
    for j in range(nseg):
        rows = slice(j * SUBLANES, (j + 1) * SUBLANES)
        pr = pw[j:j + 1, :S]
        pi = pw[j:j + 1, S:]
        e_scr[rows, :S] = e_scr[rows, :S] + (pr * einr - pi * eini)
        e_scr[rows, S:] = e_scr[rows, S:] + (pr * eini + pi * einr)

    r_scr[:, :CW] = r_scr[:, :CW] + _dot(e_scr[...].astype(BF16), wc_ref[0])

    d = d_ref[...]
    for j in range(nseg):
        rows = slice(j * SUBLANES, (j + 1) * SUBLANES)
        for i in range(CHUNK):
            cols = slice(i * LANES, (i + 1) * LANES)
            y_ref[pl.ds(CHUNK * j + i, SUBLANES, stride=seg_stride), :] = (
                r_scr[rows, cols] + d * lhs_scr[rows, cols])


def _s5(u, w1, wc, a8, pw, d, *, batch, seq, tm):
    T, ssm_w = u.shape
    nseg = tm // (SUBLANES * CHUNK)
    nc = tm // CHUNK
    CW = CHUNK * LANES
    S2 = 2 * OCT_STATE
    n_oct = ssm_w // LANES
    tiles = seq // tm
    return pl.pallas_call(
        functools.partial(_s5_kernel, tm=tm, nseg=nseg),
        grid=(n_oct, batch, tiles),
        in_specs=[
            pl.BlockSpec((tm, LANES), lambda o, b, t: (b * tiles + t, o)),
            pl.BlockSpec((1, CW, CW + S2), lambda o, b, t: (o, 0, 0)),
            pl.BlockSpec((1, S2, CW), lambda o, b, t: (o, 0, 0)),
            pl.BlockSpec((1, 1, S2), lambda o, b, t: (o, 0, 0)),
            pl.BlockSpec((1, nseg + 1, S2), lambda o, b, t: (o, 0, 0)),
            pl.BlockSpec((1, LANES), lambda o, b, t: (0, o)),
        ],
        out_specs=pl.BlockSpec((tm, LANES), lambda o, b, t: (b * tiles + t, o)),
        out_shape=jax.ShapeDtypeStruct((T, ssm_w), F32),
        scratch_shapes=[
            pltpu.VMEM((nc, CW), F32),
            pltpu.VMEM((nc, CW + S2), F32),
            pltpu.VMEM((nc, S2), F32),
            pltpu.VMEM((SUBLANES, S2), F32),
            pltpu.VMEM((SUBLANES, S2), F32),
        ],
        compiler_params=pltpu.CompilerParams(
            dimension_semantics=("arbitrary", "arbitrary", "arbitrary"), vmem_limit_bytes=VMEM_LIMIT_BYTES),
        name="s5_scan",
    )(u, w1, wc, a8, pw, d)


def _s5_tables(a_re, a_im, b_re, b_im, c_re, c_im, log_dt, nseg):
    G, P = a_re.shape
    H = b_re.shape[-1]
    n_oct = G // OCTET
    dt = jnp.exp(log_dt)[:, None]
    lam_r = a_re * dt
    lam_i = a_im * dt

    def powers(n):
        n = n.astype(F32)[:, None, None]
        mag = jnp.exp(lam_r * n)
        return mag * jnp.cos(lam_i * n), mag * jnp.sin(lam_i * n)

    pr, pi = powers(jnp.arange(CHUNK + 1))
    den = a_re * a_re + a_im * a_im
    nr = pr[1] - 1.0
    ni = pi[1]
    fr = ((nr * a_re + ni * a_im) / den)[..., None]
    fi = ((ni * a_re - nr * a_im) / den)[..., None]
    bbr = fr * b_re - fi * b_im
    bbi = fr * b_im + fi * b_re

    def octet_rows(x):
        lead = x.shape[:-3]
        x = x.reshape(lead + (n_oct, OCTET * H, P))
        x = jnp.concatenate([x, x], axis=-1)
        return jnp.moveaxis(x, len(lead), 0)

    bt = octet_rows(jnp.stack([bbr, bbi]).transpose(0, 1, 3, 2))
    ct = octet_rows(jnp.stack([c_re, c_im]))
    pwr = jnp.broadcast_to(jnp.stack([pr, pi], axis=1)[:, :, :, None, :], (CHUNK + 1, 2, G, H, P))
    pwr = octet_rows(pwr).reshape(n_oct, 2 * (CHUNK + 1), LANES, LANES)
    w1, w_c = _s5_table_call(pwr, bt, ct)

    a8 = jnp.concatenate([pr[CHUNK].reshape(n_oct, 1, OCT_STATE), pi[CHUNK].reshape(n_oct, 1, OCT_STATE)], axis=2)
    qr, qi = powers(CHUNK * jnp.arange(nseg + 1))
    pw = jnp.concatenate([qr.reshape(nseg + 1, n_oct, OCT_STATE), qi.reshape(nseg + 1, n_oct, OCT_STATE)], axis=2)
    return w1, w_c, a8, pw.transpose(1, 0, 2)


def _s5_table_kernel(pw_ref, bt_ref, ct_ref, w1_ref, wc_ref):
    S = OCT_STATE
    CW = CHUNK * LANES
    row = lax.broadcasted_iota(jnp.int32, (LANES, LANES), 0)
    lane = lax.broadcasted_iota(jnp.int32, (LANES, LANES), 1)
    same_group = (row // SSM_GROUP) == (lane // SSM_GROUP)
    first_copy = lane < SSM_STATE
    wide_row = lax.broadcasted_iota(jnp.int32, (LANES, S), 0)
    wide_lane = lax.broadcasted_iota(jnp.int32, (LANES, S), 1)
    own_states = (wide_row // SSM_GROUP) == (wide_lane // SSM_STATE)

    def spread(x):
        return jnp.where(own_states, jnp.concatenate([x] * (S // LANES), axis=1), 0.0)

    def nt_dot(a, b):
        return lax.dot_general(a, b, (((1,), (1,)), ((), ())), precision=lax.Precision.HIGHEST,
                               preferred_element_type=F32)

    btr, bti = bt_ref[0, 0], bt_ref[0, 1]
    ctr, cti = ct_ref[0, 0], ct_ref[0, 1]
    ctr_once = jnp.where(first_copy, ctr, 0.0)
    cti_once = jnp.where(first_copy, cti, 0.0)
    zero_tile = jnp.zeros((LANES, LANES), BF16)
    for n in range(CHUNK):
        pr, pi = pw_ref[0, 2 * n], pw_ref[0, 2 * n + 1]
        xr = btr * pr - bti * pi
        xi = btr * pi + bti * pr
        kern = jnp.where(same_group, nt_dot(xr, ctr_once) - nt_dot(xi, cti_once), 0.0).astype(BF16)
        for i in range(CHUNK - n):
            w1_ref[0, i * LANES:(i + 1) * LANES, (i + n) * LANES:(i + n + 1) * LANES] = kern
        for i in range(n, CHUNK) if n else ():
            w1_ref[0, i * LANES:(i + 1) * LANES, (i - n) * LANES:(i - n + 1) * LANES] = zero_tile
        i = CHUNK - 1 - n
        w1_ref[0, i * LANES:(i + 1) * LANES, CW:CW + S] = spread(xr).astype(BF16)
        w1_ref[0, i * LANES:(i + 1) * LANES, CW + S:] = spread(xi).astype(BF16)
    for j in range(CHUNK):
        pr, pi = pw_ref[0, 2 * (j + 1)], pw_ref[0, 2 * (j + 1) + 1]
        wc_ref[0, :S, j * LANES:(j + 1) * LANES] = spread(ctr * pr - cti * pi).T.astype(BF16)
        wc_ref[0, S:, j * LANES:(j + 1) * LANES] = spread(-(ctr * pi + cti * pr)).T.astype(BF16)


def _s5_table_call(pwr, bt, ct):
    n_oct = pwr.shape[0]
    CW = CHUNK * LANES
    S2 = 2 * OCT_STATE
    tile4 = lambda n: pl.BlockSpec((1, n, LANES, LANES), lambda o: (o, 0, 0, 0))
    return pl.pallas_call(
        _s5_table_kernel,
        grid=(n_oct,),
        in_specs=[tile4(pwr.shape[1]), tile4(2), tile4(2)],
        out_specs=[pl.BlockSpec((1, CW, CW + S2), lambda o: (o, 0, 0)),
                   pl.BlockSpec((1, S2, CW), lambda o: (o, 0, 0))],
        out_shape=[jax.ShapeDtypeStruct((n_oct, CW, CW + S2), BF16),
                   jax.ShapeDtypeStruct((n_oct, S2, CW), BF16)],
        compiler_params=pltpu.CompilerParams(
            dimension_semantics=("arbitrary",), vmem_limit_bytes=VMEM_LIMIT_BYTES),
        name="s5_tables",
    )(pwr, bt, ct)


KV_BLOCKS_PER_STEP = 2


def _moba_kernel(q_ref, k_ref, vt_ref, km_ref, o_ref, qt_scr, s0_scr, s1_scr, smax0_scr, smax1_scr, m_scr, acc_scr,
                 *, nblocks):
    t = pl.program_id(2)
    BK = MOBA_BLOCK
    NB = KV_BLOCKS_PER_STEP
    QT = q_ref.shape[0]
    q2t = q_ref[...].astype(F32).T
    feat = lax.broadcasted_iota(jnp.int32, q2t.shape, 0)
    km = km_ref[0]
    km_hi = km.astype(BF16)
    km_lo = (km - km_hi.astype(F32)).astype(BF16)
    blk = lax.broadcasted_iota(jnp.int32, (nblocks, QT), 0)
    blk_f = blk.astype(F32)
    own = t * (QT // BK) + lax.broadcasted_iota(jnp.int32, (nblocks, QT), 1) // BK
    pad = jnp.zeros((LANES - nblocks, QT), BF16)

    for a in range(2):
        qat = jnp.where((feat >= a * HEAD_DIM) & (feat < (a + 1) * HEAD_DIM), q2t, 0.0).astype(BF16)
        gate = _dot(km_hi, qat) + _dot(km_lo, qat)
        gate = jnp.where(blk < own, gate, NEG_INF)
        sel = jnp.zeros(gate.shape, jnp.bool_)
        for _ in range(MOBA_TOPK):
            top = jnp.max(gate, axis=0, keepdims=True)
            idx = jnp.min(jnp.where(gate == top, blk_f, float(nblocks)), axis=0, keepdims=True)
            pick = blk_f == idx
            sel = sel | pick
            gate = jnp.where(pick, -jnp.inf, gate)
        bias = jnp.where((sel & (blk < own)) | (blk == own), 0.0, NEG_INF).astype(BF16)
        qt_scr[a] = jnp.concatenate([qat, bias, pad], axis=0)
        m_scr[a] = jnp.full((1, QT), NEG_INF, F32)
        acc_scr[a] = jnp.zeros((VT_ROWS, QT), F32)

    all_q = slice(0, QT)
    late_q = slice(NB * BK, QT)

    def scores(c, s_scr, smax_scr, qs=all_q):
        kb = k_ref[pl.ds(pl.multiple_of(c * (NB * BK), NB * BK), NB * BK), :]
        for a in range(2):
            s = _dot(kb, qt_scr[a, :, qs])
            s_scr[a, :, qs] = s
            smax_scr[a, :, qs] = jnp.max(s, axis=0, keepdims=True)

    def consume(c, s_scr, smax_scr, qs=all_q):
        for a in range(2):
            vta = jnp.concatenate(
                [vt_ref[c * NB + j, a * VT_ROWS:(a + 1) * VT_ROWS, :] for j in range(NB)], axis=1)
            m_old = m_scr[a, :, qs]
            m_new = jnp.maximum(m_old, smax_scr[a, :, qs])
            p = jnp.exp2(s_scr[a, :, qs] - m_new).astype(BF16)
            acc_scr[a, :, qs] = jnp.exp2(m_old - m_new) * acc_scr[a, :, qs] + _dot(vta, p)
            m_scr[a, :, qs] = m_new

    tri = (lax.broadcasted_iota(jnp.int32, (BK, BK), 0) <= lax.broadcasted_iota(jnp.int32, (BK, BK), 1))

    def causal_patch(e, s_scr, smax_scr, qs=all_q):
        for a in range(2):
            for j in range(NB):
                rows = slice(j * BK, (j + 1) * BK)
                cols = slice((e * NB + j) * BK, (e * NB + j + 1) * BK)
                s_scr[a, rows, cols] = jnp.where(tri, s_scr[a, rows, cols], NEG_INF)
            smax_scr[a, :, qs] = jnp.max(s_scr[a, :, qs], axis=0, keepdims=True)

    buf0 = (s0_scr, smax0_scr)
    buf1 = (s1_scr, smax1_scr)
    scores(0, *buf0)

    def body(i, carry):
        scores(2 * i + 1, *buf1)
        consume(2 * i, *buf0)
        scores(2 * i + 2, *buf0)
        consume(2 * i + 1, *buf1)
        return carry

    lax.fori_loop(0, t, body, 0)
    scores(2 * t + 1, *buf1, qs=late_q)
    causal_patch(0, *buf0)
    consume(2 * t, *buf0)
    causal_patch(1, *buf1, qs=late_q)
    consume(2 * t + 1, *buf1, qs=late_q)
    ot = jnp.concatenate([acc_scr[a, :HEAD_DIM] / acc_scr[a, HEAD_DIM:HEAD_DIM + 1] for a in range(2)], axis=0)
    o_ref[...] = ot.T.astype(BF16)


def _moba(q, k, vt, kmean, *, batch, seq):
    T, attn_w = q.shape
    nblocks = seq // MOBA_BLOCK
    n_pairs = attn_w // LANES
    BK = MOBA_BLOCK
    QT = 2 * KV_BLOCKS_PER_STEP * BK
    tiles = seq // QT
    return pl.pallas_call(
        functools.partial(_moba_kernel, nblocks=nblocks),
        grid=(batch, n_pairs, tiles),
        in_specs=[
            pl.BlockSpec((QT, LANES), lambda b, h, i: (b * tiles + i, h)),
            pl.BlockSpec((seq, 2 * LANES), lambda b, h, i: (b, h)),
            pl.BlockSpec((nblocks, 2 * VT_ROWS, BK), lambda b, h, i: (b, h, 0)),
            pl.BlockSpec((1, nblocks, LANES), lambda b, h, i: (b, 0, h)),
        ],
        out_specs=pl.BlockSpec((QT, LANES), lambda b, h, i: (b * tiles + i, h)),
        out_shape=jax.ShapeDtypeStruct((T, attn_w), BF16),
        scratch_shapes=[
            pltpu.VMEM((2, 2 * LANES, QT), BF16),
            pltpu.VMEM((2, KV_BLOCKS_PER_STEP * BK, QT), F32),
            pltpu.VMEM((2, KV_BLOCKS_PER_STEP * BK, QT), F32),
            pltpu.VMEM((2, 1, QT), F32),
            pltpu.VMEM((2, 1, QT), F32),
            pltpu.VMEM((2, 1, QT), F32),
            pltpu.VMEM((2, VT_ROWS, QT), F32),
        ],
        compiler_params=pltpu.CompilerParams(
            dimension_semantics=("arbitrary", "arbitrary", "arbitrary"), vmem_limit_bytes=VMEM_LIMIT_BYTES),
        name="moba_attn",
    )(q, k, vt, kmean)


def _out_ffn_kernel(x1_ref, y_ref, a_ref, gm_ref, wgate_ref, gluw_ref, glub_ref, wbs_ref, wba_ref, wout_ref,
                    g3_ref, wg_ref, wu_ref, wd_ref, gf_ref, o_ref):
    x1 = x1_ref[...]
    D = x1.shape[1]
    h = _rmsnorm(x1, gm_ref[...]).astype(BF16)
    gates = _dot(h, wgate_ref[...])
    ys = jax.nn.gelu(y_ref[...])
    ys = ys * jax.nn.sigmoid(_dot(ys.astype(BF16), gluw_ref[...]) + glub_ref[...])
    branch_a = _dot(ys.astype(BF16), wbs_ref[...])
    branch_b = _dot(a_ref[...], wba_ref[...])
    merged = jax.nn.sigmoid(gates[:, :D]) * branch_a + jax.nn.sigmoid(gates[:, D:]) * branch_b
    x2 = x1 + _dot(merged.astype(BF16), wout_ref[...])

    h3 = _rmsnorm(x2, g3_ref[...]).astype(BF16)
    act = (jax.nn.silu(_dot(h3, wg_ref[...])) * _dot(h3, wu_ref[...])).astype(BF16)
    x3 = x2 + 0.5 * _dot(act, wd_ref[...])
    o_ref[...] = _rmsnorm(x3, gf_ref[...])


def _out_ffn(x1, y, attn, gm, wgate, gluw, glub, wbs, wba, wout, g3, wg, wu, wd, gf, *, tm):
    T, D = x1.shape
    F = wg.shape[1]
    ssm_w = y.shape[1]
    attn_w = attn.shape[1]
    row = lambda i: (i, 0)
    return pl.pallas_call(
        _out_ffn_kernel,
        grid=(T // tm,),
        in_specs=[
            pl.BlockSpec((tm, D), row), pl.BlockSpec((tm, ssm_w), row), pl.BlockSpec((tm, attn_w), row),
            _const_spec((1, D)), _const_spec((D, 2 * D)), _const_spec((ssm_w, ssm_w)), _const_spec((1, ssm_w)),
            _const_spec((ssm_w, D)), _const_spec((attn_w, D)), _const_spec((D, D)),
            _const_spec((1, D)), _const_spec((D, F)), _const_spec((D, F)), _const_spec((F, D)),
            _const_spec((1, D)),
        ],
        out_specs=pl.BlockSpec((tm, D), row),
        out_shape=jax.ShapeDtypeStruct((T, D), F32),
        compiler_params=pltpu.CompilerParams(
            dimension_semantics=("arbitrary",), vmem_limit_bytes=VMEM_LIMIT_BYTES),
        name="out_ffn2",
    )(x1, y, attn, gm, wgate, gluw, glub, wbs, wba, wout, g3, wg, wu, wd, gf)


def _rope_tables(seq):
    pos = jnp.arange(seq, dtype=F32)
    inv_freq = ROPE_THETA ** (-jnp.arange(0, HEAD_DIM, 2, dtype=F32) / HEAD_DIM)
    ang = pos[:, None] * inv_freq[None, :]
    cos = jnp.cos(ang)
    sin = jnp.sin(ang)
    cos_t = jnp.tile(cos, (1, 2 * LANES // HEAD_DIM))
    sin_t = jnp.tile(jnp.concatenate([-sin, sin], axis=1), (1, LANES // HEAD_DIM))
    return cos_t, sin_t


TM_FFN = 512
TM_S5 = 2048


def kernel(x, ffn1_norm, ffn1_w_gate, ffn1_w_up, ffn1_w_down, mix_norm, w_in, ssm_a_re, ssm_a_im, ssm_b_re, ssm_b_im, ssm_c_re, ssm_c_im, ssm_d, ssm_log_dt, glu_w, glu_b, w_branch_ssm, w_branch_attn, w_out, ffn2_norm, ffn2_w_gate, ffn2_w_up, ffn2_w_down, final_norm):
    B, L, D = x.shape
    ssm_w = glu_w.shape[1]
    attn_w = w_branch_attn.shape[1]
    T = B * L
    assert ffn1_norm.shape[0] == 1, "single-layer trunk only"
    assert attn_w == N_HEADS * HEAD_DIM and L % TM_S5 == 0 and T % TM_FFN == 0
    assert TM_FFN % MOBA_BLOCK == 0 and L % TM_FFN == 0
    bf = lambda w: w[0].astype(BF16)
    row = lambda v: v[0][None]
    cos_t, sin_t = _rope_tables(L)
    nseg = TM_S5 // (SUBLANES * CHUNK)
    wi = w_in[0]
    off_v = ssm_w + 2 * attn_w
    off_g = ssm_w + 3 * attn_w
    x1, u, q, k, vt, kmean8 = _ffn_inproj(
        x.reshape(T, D), row(ffn1_norm), bf(ffn1_w_gate), bf(ffn1_w_up), bf(ffn1_w_down), row(mix_norm),
        wi[:, :off_v].astype(BF16), wi[:, off_v:off_g].T.astype(BF16), cos_t, sin_t, seq=L, tm=TM_FFN)
    w1, wc, a8, pw = _s5_tables(ssm_a_re[0], ssm_a_im[0], ssm_b_re[0], ssm_b_im[0],
                                ssm_c_re[0], ssm_c_im[0], ssm_log_dt[0], nseg)
    y = _s5(u, w1, wc, a8, pw, row(ssm_d), batch=B, seq=L, tm=TM_S5)
    kmean = kmean8[::SUBLANES].reshape(B, L // MOBA_BLOCK, attn_w)
    attn = _moba(q, k, vt, kmean, batch=B, seq=L)
    out = _out_ffn(
        x1, y, attn, row(mix_norm), wi[:, off_g:].astype(BF16), bf(glu_w), row(glu_b), bf(w_branch_ssm),
        bf(w_branch_attn), bf(w_out), row(ffn2_norm), bf(ffn2_w_gate), bf(ffn2_w_up), bf(ffn2_w_down),
        final_norm[None], tm=TM_FFN)
    return out.reshape(B, L, D)
```

```python
import functools
import math

import jax
import jax.numpy as jnp
from jax import lax
from jax.experimental import pallas as pl
from jax.experimental.pallas import tpu as pltpu

F32 = jnp.float32
BF16 = jnp.bfloat16

N_HEADS = 8
HEAD_DIM = 64
MOBA_BLOCK = 256
MOBA_TOPK = 3
SSM_GROUP = 16
SSM_STATE = 64
ROPE_THETA = 10000.0
RMS_EPS = 1e-6
NEG_INF = -1e30

LANES = 128
SUBLANES = 8
VMEM_LIMIT_BYTES = 56 * 1024 * 1024

CHUNK = 8
OCTET = LANES // SSM_GROUP
OCT_STATE = OCTET * SSM_STATE

VT_ROWS = HEAD_DIM + 16


def _rmsnorm(x, gain):
    inv = lax.rsqrt(jnp.mean(x * x, axis=-1, keepdims=True) + RMS_EPS)
    return (x * inv) * gain


def _dot(a, b):
    return jnp.dot(a, b, preferred_element_type=F32)


def _const_spec(shape):
    nd = len(shape)
    return pl.BlockSpec(shape, lambda *_: (0,) * nd, pipeline_mode=pl.Buffered(1))


def _rotary_tile(x, cos, sin_signed, first_half):
    swapped = jnp.where(first_half, pltpu.roll(x, LANES - HEAD_DIM // 2, 1), pltpu.roll(x, HEAD_DIM // 2, 1))
    return x * cos + swapped * sin_signed


def _ffn_inproj_kernel(x_ref, g1_ref, wg_ref, wu_ref, wd_ref, g2_ref, win_ref, wvt_ref, cos_ref, sin_ref,
                       x1_ref, u_ref, q_ref, k_ref, vt_ref, kmean_ref, *, attn_w, ssm_w, nblocks):
    x = x_ref[...]
    h = _rmsnorm(x, g1_ref[...]).astype(BF16)
    act = (jax.nn.silu(_dot(h, wg_ref[...])) * _dot(h, wu_ref[...])).astype(BF16)
    x1 = x + 0.5 * _dot(act, wd_ref[...])
    x1_ref[...] = x1

    h2 = _rmsnorm(x1, g2_ref[...]).astype(BF16)
    proj = _dot(h2, win_ref[...])
    u_ref[...] = proj[:, :ssm_w]

    cos = cos_ref[...]
    sin_signed = sin_ref[...]
    lane = lax.broadcasted_iota(jnp.int32, cos.shape, 1)
    first_half = (lane % HEAD_DIM) < (HEAD_DIM // 2)
    scale = HEAD_DIM ** -0.5 * math.log2(math.e)
    tm = x.shape[0]
    nblk = tm // MOBA_BLOCK
    key_blk = (pl.program_id(0) * nblk + lax.broadcasted_iota(jnp.int32, cos.shape, 0) // MOBA_BLOCK) % nblocks
    blk_onehot = (lane == key_blk).astype(BF16)
    for t in range(attn_w // LANES):
        qs = proj[:, ssm_w + t * LANES: ssm_w + (t + 1) * LANES]
        ks = proj[:, ssm_w + attn_w + t * LANES: ssm_w + attn_w + (t + 1) * LANES]
        q_ref[:, t * LANES:(t + 1) * LANES] = (_rotary_tile(qs, cos, sin_signed, first_half) * scale).astype(BF16)
        kr = _rotary_tile(ks, cos, sin_signed, first_half)
        k_ref[:, 2 * t * LANES:(2 * t + 1) * LANES] = kr.astype(BF16)
        k_ref[:, (2 * t + 1) * LANES:(2 * t + 2) * LANES] = blk_onehot
        for j in range(nblk):
            mean = jnp.mean(kr[j * MOBA_BLOCK:(j + 1) * MOBA_BLOCK], axis=0, keepdims=True)
            kmean_ref[j * SUBLANES:(j + 1) * SUBLANES, t * LANES:(t + 1) * LANES] = jnp.broadcast_to(
                mean, (SUBLANES, LANES))

    vt = lax.dot_general(wvt_ref[...], h2, (((1,), (1,)), ((), ())), preferred_element_type=F32).astype(BF16)
    ones = jnp.ones((VT_ROWS - HEAD_DIM, MOBA_BLOCK), BF16)
    for j in range(nblk):
        for hd in range(attn_w // HEAD_DIM):
            vt_ref[j, hd * VT_ROWS:hd * VT_ROWS + HEAD_DIM, :] = (
                vt[hd * HEAD_DIM:(hd + 1) * HEAD_DIM, j * MOBA_BLOCK:(j + 1) * MOBA_BLOCK])
            vt_ref[j, hd * VT_ROWS + HEAD_DIM:(hd + 1) * VT_ROWS, :] = ones


def _ffn_inproj(x2d, g1, wg, wu, wd, g2, win_uqk, wv_t, cos_t, sin_t, *, seq, tm):
    T, D = x2d.shape
    F = wg.shape[1]
    attn_w = wv_t.shape[0]
    ssm_w = win_uqk.shape[1] - 2 * attn_w
    nblk = tm // MOBA_BLOCK
    tiles_per_seq = seq // tm
    row = lambda i: (i, 0)
    return pl.pallas_call(
        functools.partial(_ffn_inproj_kernel, attn_w=attn_w, ssm_w=ssm_w, nblocks=seq // MOBA_BLOCK),
        grid=(T // tm,),
        in_specs=[
            pl.BlockSpec((tm, D), row),
            _const_spec((1, D)), _const_spec((D, F)), _const_spec((D, F)), _const_spec((F, D)),
            _const_spec((1, D)), _const_spec((D, ssm_w + 2 * attn_w)), _const_spec((attn_w, D)),
            pl.BlockSpec((tm, LANES), lambda i: (i % tiles_per_seq, 0)),
            pl.BlockSpec((tm, LANES), lambda i: (i % tiles_per_seq, 0)),
        ],
        out_specs=[
            pl.BlockSpec((tm, D), row),
            pl.BlockSpec((tm, ssm_w), row),
            pl.BlockSpec((tm, attn_w), row),
            pl.BlockSpec((tm, 2 * attn_w), row),
            pl.BlockSpec((nblk, attn_w // HEAD_DIM * VT_ROWS, MOBA_BLOCK), lambda i: (i, 0, 0)),
            pl.BlockSpec((nblk * SUBLANES, attn_w), row),
        ],
        out_shape=[
            jax.ShapeDtypeStruct((T, D), F32),
            jax.ShapeDtypeStruct((T, ssm_w), F32),
            jax.ShapeDtypeStruct((T, attn_w), BF16),
            jax.ShapeDtypeStruct((T, 2 * attn_w), BF16),
            jax.ShapeDtypeStruct((T // MOBA_BLOCK, attn_w // HEAD_DIM * VT_ROWS, MOBA_BLOCK), BF16),
            jax.ShapeDtypeStruct((T // MOBA_BLOCK * SUBLANES, attn_w), F32),
        ],
        compiler_params=pltpu.CompilerParams(
            dimension_semantics=("arbitrary",), vmem_limit_bytes=VMEM_LIMIT_BYTES),
        name="ffn1_inproj",
    )(x2d, g1, wg, wu, wd, g2, win_uqk, wv_t, cos_t, sin_t)


def _s5_kernel(u_ref, w1_ref, wc_ref, a8_ref, pw_ref, d_ref, y_ref,
               lhs_scr, z_scr, y_scr, e_scr, ein_scr, carry_scr, *, tm, nseg):
    S = OCT_STATE
    CW = CHUNK * LANES
    seg_stride = tm // SUBLANES

    @pl.when(pl.program_id(2) == 0)
    def _():
        carry_scr[...] = jnp.zeros_like(carry_scr)

    for j in range(nseg):
        for i in range(CHUNK):
            lhs_scr[j * SUBLANES:(j + 1) * SUBLANES, i * LANES:(i + 1) * LANES] = (
                u_ref[pl.ds(CHUNK * j + i, SUBLANES, stride=seg_stride), :])

    lhs = lhs_scr[...].astype(BF16)
    z_scr[...] = _dot(lhs, w1_ref[0, :, CW:])
    y_scr[...] = _dot(lhs, w1_ref[0, :, :CW])

    a8 = a8_ref[0]
    ar = jnp.broadcast_to(a8[:, :S], (SUBLANES, S))
    ai = jnp.broadcast_to(a8[:, S:], (SUBLANES, S))
    er = jnp.zeros((SUBLANES, S), F32)
    ei = jnp.zeros((SUBLANES, S), F32)
    for j in range(nseg):
        rows = slice(j * SUBLANES, (j + 1) * SUBLANES)
        e_scr[rows, :S] = er
        e_scr[rows, S:] = ei
        zr = z_scr[rows, :S]
        zi = z_scr[rows, S:]
        er, ei = ar * er - ai * ei + zr, ar * ei + ai * er + zi
    ein_scr[:, :S] = er
    ein_scr[:, S:] = ei

    pw = pw_ref[0]
    pnr = pw[nseg:nseg + 1, :S]
    pni = pw[nseg:nseg + 1, S:]
    cr = carry_scr[0:1, :S]
    ci = carry_scr[0:1, S:]
    for s in range(SUBLANES):
        fr = ein_scr[s:s + 1, :S]
        fi = ein_scr[s:s + 1, S:]
        ein_scr[s:s + 1, :S] = cr
        ein_scr[s:s + 1, S:] = ci
        cr, ci = fr + pnr * cr - pni * ci, fi + pnr * ci + pni * cr
    carry_scr[0:1, :S] = cr
    carry_scr[0:1, S:] = ci

    einr = ein_scr[:, :S]
    eini = ein_scr[:, S:]
    for j in range(nseg):
        rows = slice(j * SUBLANES, (j + 1) * SUBLANES)
        pr = pw[j:j + 1, :S]
        pi = pw[j:j + 1, S:]
        e_scr[rows, :S] = e_scr[rows, :S] + (pr * einr - pi * eini)
        e_scr[rows, S:] = e_scr[rows, S:] + (pr * eini + pi * einr)

    y_scr[...] = y_scr[...] + _dot(e_scr[...].astype(BF16), wc_ref[0])

    d = d_ref[...]
    for j in range(nseg):
        rows = slice(j * SUBLANES, (j + 1) * SUBLANES)
        for i in range(CHUNK):
            cols = slice(i * LANES, (i + 1) * LANES)
            y_ref[pl.ds(CHUNK * j + i, SUBLANES, stride=seg_stride), :] = (
                y_scr[rows, cols] + d * lhs_scr[rows, cols])


def _s5(u, w1, wc, a8, pw, d, *, batch, seq, tm):
    T, ssm_w = u.shape
    nseg = tm // (SUBLANES * CHUNK)
    nc = tm // CHUNK
    CW = CHUNK * LANES
    S2 = 2 * OCT_STATE
    n_oct = ssm_w // LANES
    tiles = seq // tm
    return pl.pallas_call(
        functools.partial(_s5_kernel, tm=tm, nseg=nseg),
        grid=(n_oct, batch, tiles),
        in_specs=[
            pl.BlockSpec((tm, LANES), lambda o, b, t: (b * tiles + t, o)),
            pl.BlockSpec((1, CW, CW + S2), lambda o, b, t: (o, 0, 0)),
            pl.BlockSpec((1, S2, CW), lambda o, b, t: (o, 0, 0)),
            pl.BlockSpec((1, 1, S2), lambda o, b, t: (o, 0, 0)),
            pl.BlockSpec((1, nseg + 1, S2), lambda o, b, t: (o, 0, 0)),
            pl.BlockSpec((1, LANES), lambda o, b, t: (0, o)),
        ],
        out_specs=pl.BlockSpec((tm, LANES), lambda o, b, t: (b * tiles + t, o)),
        out_shape=jax.ShapeDtypeStruct((T, ssm_w), F32),
        scratch_shapes=[
            pltpu.VMEM((nc, CW), F32),
            pltpu.VMEM((nc, S2), F32),
            pltpu.VMEM((nc, CW), F32),
            pltpu.VMEM((nc, S2), F32),
            pltpu.VMEM((SUBLANES, S2), F32),
            pltpu.VMEM((SUBLANES, S2), F32),
        ],
        compiler_params=pltpu.CompilerParams(
            dimension_semantics=("arbitrary", "arbitrary", "arbitrary"), vmem_limit_bytes=VMEM_LIMIT_BYTES),
        name="s5_scan",
    )(u, w1, wc, a8, pw, d)


def _s5_tables(a_re, a_im, b_re, b_im, c_re, c_im, log_dt, nseg):
    G, P = a_re.shape
    H = b_re.shape[-1]
    n_oct = G // OCTET
    dt = jnp.exp(log_dt)[:, None]
    lam_r = a_re * dt
    lam_i = a_im * dt

    def powers(n):
        n = n.astype(F32)[:, None, None]
        mag = jnp.exp(lam_r * n)
        return mag * jnp.cos(lam_i * n), mag * jnp.sin(lam_i * n)

    pr, pi = powers(jnp.arange(CHUNK + 1))
    den = a_re * a_re + a_im * a_im
    nr = pr[1] - 1.0
    ni = pi[1]
    fr = ((nr * a_re + ni * a_im) / den)[..., None]
    fi = ((ni * a_re - nr * a_im) / den)[..., None]
    bbr = fr * b_re - fi * b_im
    bbi = fr * b_im + fi * b_re

    def octet_rows(x):
        lead = x.shape[:-3]
        x = x.reshape(lead + (n_oct, OCTET * H, P))
        x = jnp.concatenate([x, x], axis=-1)
        return jnp.moveaxis(x, len(lead), 0)

    bt = octet_rows(jnp.stack([bbr, bbi]).transpose(0, 1, 3, 2))
    ct = octet_rows(jnp.stack([c_re, c_im]))
    pwr = jnp.broadcast_to(jnp.stack([pr, pi], axis=1)[:, :, :, None, :], (CHUNK + 1, 2, G, H, P))
    pwr = octet_rows(pwr).reshape(n_oct, 2 * (CHUNK + 1), LANES, LANES)
    w1, w_c = _s5_table_call(pwr, bt, ct)

    a8 = jnp.concatenate([pr[CHUNK].reshape(n_oct, 1, OCT_STATE), pi[CHUNK].reshape(n_oct, 1, OCT_STATE)], axis=2)
    qr, qi = powers(CHUNK * jnp.arange(nseg + 1))
    pw = jnp.concatenate([qr.reshape(nseg + 1, n_oct, OCT_STATE), qi.reshape(nseg + 1, n_oct, OCT_STATE)], axis=2)
    return w1, w_c, a8, pw.transpose(1, 0, 2)


def _s5_table_kernel(pw_ref, bt_ref, ct_ref, w1_ref, wc_ref):
    S = OCT_STATE
    CW = CHUNK * LANES
    row = lax.broadcasted_iota(jnp.int32, (LANES, LANES), 0)
    lane = lax.broadcasted_iota(jnp.int32, (LANES, LANES), 1)
    same_group = (row // SSM_GROUP) == (lane // SSM_GROUP)
    first_copy = lane < SSM_STATE
    wide_row = lax.broadcasted_iota(jnp.int32, (LANES, S), 0)
    wide_lane = lax.broadcasted_iota(jnp.int32, (LANES, S), 1)
    own_states = (wide_row // SSM_GROUP) == (wide_lane // SSM_STATE)

    def spread(x):
        return jnp.where(own_states, jnp.concatenate([x] * (S // LANES), axis=1), 0.0)

    def nt_dot(a, b):
        return lax.dot_general(a, b, (((1,), (1,)), ((), ())), precision=lax.Precision.HIGHEST,
                               preferred_element_type=F32)

    btr, bti = bt_ref[0, 0], bt_ref[0, 1]
    ctr, cti = ct_ref[0, 0], ct_ref[0, 1]
    ctr_once = jnp.where(first_copy, ctr, 0.0)
    cti_once = jnp.where(first_copy, cti, 0.0)
    zero_tile = jnp.zeros((LANES, LANES), BF16)
    for n in range(CHUNK):
        pr, pi = pw_ref[0, 2 * n], pw_ref[0, 2 * n + 1]
        xr = btr * pr - bti * pi
        xi = btr * pi + bti * pr
        kern = jnp.where(same_group, nt_dot(xr, ctr_once) - nt_dot(xi, cti_once), 0.0).astype(BF16)
        for i in range(CHUNK - n):
            w1_ref[0, i * LANES:(i + 1) * LANES, (i + n) * LANES:(i + n + 1) * LANES] = kern
        for i in range(n, CHUNK) if n else ():
            w1_ref[0, i * LANES:(i + 1) * LANES, (i - n) * LANES:(i - n + 1) * LANES] = zero_tile
        i = CHUNK - 1 - n
        w1_ref[0, i * LANES:(i + 1) * LANES, CW:CW + S] = spread(xr).astype(BF16)
        w1_ref[0, i * LANES:(i + 1) * LANES, CW + S:] = spread(xi).astype(BF16)
    for j in range(CHUNK):
        pr, pi = pw_ref[0, 2 * (j + 1)], pw_ref[0, 2 * (j + 1) + 1]
        wc_ref[0, :S, j * LANES:(j + 1) * LANES] = spread(ctr * pr - cti * pi).T.astype(BF16)
        wc_ref[0, S:, j * LANES:(j + 1) * LANES] = spread(-(ctr * pi + cti * pr)).T.astype(BF16)


def _s5_table_call(pwr, bt, ct):
    n_oct = pwr.shape[0]
    CW = CHUNK * LANES
    S2 = 2 * OCT_STATE
    tile4 = lambda n: pl.BlockSpec((1, n, LANES, LANES), lambda o: (o, 0, 0, 0))
    return pl.pallas_call(
        _s5_table_kernel,
        grid=(n_oct,),
        in_specs=[tile4(pwr.shape[1]), tile4(2), tile4(2)],
        out_specs=[pl.BlockSpec((1, CW, CW + S2), lambda o: (o, 0, 0)),
                   pl.BlockSpec((1, S2, CW), lambda o: (o, 0, 0))],
        out_shape=[jax.ShapeDtypeStruct((n_oct, CW, CW + S2), BF16),
                   jax.ShapeDtypeStruct((n_oct, S2, CW), BF16)],
        compiler_params=pltpu.CompilerParams(
            dimension_semantics=("arbitrary",), vmem_limit_bytes=VMEM_LIMIT_BYTES),
        name="s5_tables",
    )(pwr, bt, ct)


KV_BLOCKS_PER_STEP = 2

def _moba_kernel(q_ref, k_ref, vt_ref, km_ref, o_ref,
                 qt_scr, s0_scr, s1_scr, smax0_scr, smax1_scr, m_scr, acc_scr, *, nblocks):
    t = pl.program_id(2)
    BK = MOBA_BLOCK
    NB = KV_BLOCKS_PER_STEP
    QT = q_ref.shape[0]

    q2t = q_ref[...].astype(F32).T.astype(BF16)
    km = km_ref[0]
    km_hi = km.astype(BF16)
    km_lo = (km - km_hi.astype(F32)).astype(BF16)
    blk = lax.broadcasted_iota(jnp.int32, (nblocks, QT), 0)
    blk_f = blk.astype(F32)
    own = t * (QT // BK) + lax.broadcasted_iota(jnp.int32, (nblocks, QT), 1) // BK
    pad = jnp.zeros((LANES - nblocks, QT), BF16)
    no_feat = jnp.zeros((HEAD_DIM, QT), BF16)
    for a in range(2):
        head = q2t[a * HEAD_DIM:(a + 1) * HEAD_DIM]
        qat = jnp.concatenate([head, no_feat] if a == 0 else [no_feat, head], axis=0)
        gate = _dot(km_hi, qat) + _dot(km_lo, qat)
        gate = jnp.where(blk < own, gate, NEG_INF)
        sel = jnp.zeros(gate.shape, jnp.bool_)
        for _ in range(MOBA_TOPK):
            top = jnp.max(gate, axis=0, keepdims=True)
            idx = jnp.min(jnp.where(gate == top, blk_f, float(nblocks)), axis=0, keepdims=True)
            pick = blk_f == idx
            sel = sel | pick
            gate = jnp.where(pick, -jnp.inf, gate)
        bias = jnp.where((sel & (blk < own)) | (blk == own), 0.0, NEG_INF).astype(BF16)
        qt_scr[a] = jnp.concatenate([qat, bias, pad], axis=0)
        m_scr[a] = jnp.full((1, QT), NEG_INF, F32)
        acc_scr[a] = jnp.zeros((VT_ROWS, QT), F32)

    all_q = slice(0, QT)
    late_q = slice(NB * BK, QT)

    def scores(c, s_scr, smax_scr, qs=all_q):
        kb = k_ref[pl.ds(pl.multiple_of(c * (NB * BK), NB * BK), NB * BK), :]
        for a in range(2):
            s = _dot(kb, qt_scr[a, :, qs])
            s_scr[a, :, qs] = s
            smax_scr[a, :, qs] = jnp.max(s, axis=0, keepdims=True)

    def consume(c, s_scr, smax_scr, qs=all_q):
        for a in range(2):
            vta = jnp.concatenate(
                [vt_ref[c * NB + j, a * VT_ROWS:(a + 1) * VT_ROWS, :] for j in range(NB)], axis=1)
            m_old = m_scr[a, :, qs]
            m_new = jnp.maximum(m_old, smax_scr[a, :, qs])
            p = jnp.exp2(s_scr[a, :, qs] - m_new).astype(BF16)
            acc_scr[a, :, qs] = jnp.exp2(m_old - m_new) * acc_scr[a, :, qs] + _dot(vta, p)
            m_scr[a, :, qs] = m_new

    tri = (lax.broadcasted_iota(jnp.int32, (BK, BK), 0) <= lax.broadcasted_iota(jnp.int32, (BK, BK), 1))

    def causal_patch(e, s_scr, smax_scr, qs=all_q):
        for a in range(2):
            for j in range(NB):
                rows = slice(j * BK, (j + 1) * BK)
                cols = slice((e * NB + j) * BK, (e * NB + j + 1) * BK)
                s_scr[a, rows, cols] = jnp.where(tri, s_scr[a, rows, cols], NEG_INF)
            smax_scr[a, :, qs] = jnp.max(s_scr[a, :, qs], axis=0, keepdims=True)

    buf0 = (s0_scr, smax0_scr)
    buf1 = (s1_scr, smax1_scr)
    scores(0, *buf0)

    def body(i, carry):
        scores(2 * i + 1, *buf1)
        consume(2 * i, *buf0)
        scores(2 * i + 2, *buf0)
        consume(2 * i + 1, *buf1)
        return carry

    lax.fori_loop(0, t, body, 0)
    scores(2 * t + 1, *buf1, qs=late_q)
    causal_patch(0, *buf0)
    consume(2 * t, *buf0)
    causal_patch(1, *buf1, qs=late_q)
    consume(2 * t + 1, *buf1, qs=late_q)
    ot = jnp.concatenate([acc_scr[a, :HEAD_DIM] / acc_scr[a, HEAD_DIM:HEAD_DIM + 1] for a in range(2)], axis=0)
    o_ref[...] = ot.T.astype(BF16)


def _moba(q, k, vt, kmean, *, batch, seq):
    T, attn_w = q.shape
    nblocks = seq // MOBA_BLOCK
    n_pairs = attn_w // LANES
    BK = MOBA_BLOCK
    QT = 2 * KV_BLOCKS_PER_STEP * BK
    tiles = seq // QT
    return pl.pallas_call(
        functools.partial(_moba_kernel, nblocks=nblocks),
        grid=(batch, n_pairs, tiles),
        in_specs=[
            pl.BlockSpec((QT, LANES), lambda b, h, i: (b * tiles + i, h)),
            pl.BlockSpec((seq, 2 * LANES), lambda b, h, i: (b, h)),
            pl.BlockSpec((nblocks, 2 * VT_ROWS, BK), lambda b, h, i: (b, h, 0)),
            pl.BlockSpec((1, nblocks, LANES), lambda b, h, i: (b, 0, h)),
        ],
        out_specs=pl.BlockSpec((QT, LANES), lambda b, h, i: (b * tiles + i, h)),
        out_shape=jax.ShapeDtypeStruct((T, attn_w), BF16),
        scratch_shapes=[
            pltpu.VMEM((2, 2 * LANES, QT), BF16),
            pltpu.VMEM((2, KV_BLOCKS_PER_STEP * BK, QT), F32),
            pltpu.VMEM((2, KV_BLOCKS_PER_STEP * BK, QT), F32),
            pltpu.VMEM((2, 1, QT), F32),
            pltpu.VMEM((2, 1, QT), F32),
            pltpu.VMEM((2, 1, QT), F32),
            pltpu.VMEM((2, VT_ROWS, QT), F32),
        ],
        compiler_params=pltpu.CompilerParams(
            dimension_semantics=("arbitrary", "arbitrary", "arbitrary"), vmem_limit_bytes=VMEM_LIMIT_BYTES),
        name="moba_attn",
    )(q, k, vt, kmean)


def _out_ffn_kernel(x1_ref, y_ref, a_ref, gm_ref, wgate_ref, gluw_ref, glub_ref, wbs_ref, wba_ref, wout_ref,
                    g3_ref, wg_ref, wu_ref, wd_ref, gf_ref, o_ref):
    x1 = x1_ref[...]
    D = x1.shape[1]
    h = _rmsnorm(x1, gm_ref[...]).astype(BF16)
    gates = _dot(h, wgate_ref[...])
    ys = jax.nn.gelu(y_ref[...])
    ys = ys * jax.nn.sigmoid(_dot(ys.astype(BF16), gluw_ref[...]) + glub_ref[...])
    branch_a = _dot(ys.astype(BF16), wbs_ref[...])
    branch_b = _dot(a_ref[...], wba_ref[...])
    merged = jax.nn.sigmoid(gates[:, :D]) * branch_a + jax.nn.sigmoid(gates[:, D:]) * branch_b
    x2 = x1 + _dot(merged.astype(BF16), wout_ref[...])

    h3 = _rmsnorm(x2, g3_ref[...]).astype(BF16)
    act = (jax.nn.silu(_dot(h3, wg_ref[...])) * _dot(h3, wu_ref[...])).astype(BF16)
    x3 = x2 + 0.5 * _dot(act, wd_ref[...])
    o_ref[...] = _rmsnorm(x3, gf_ref[...])


def _out_ffn(x1, y, attn, gm, wgate, gluw, glub, wbs, wba, wout, g3, wg, wu, wd, gf, *, tm):
    T, D = x1.shape
    F = wg.shape[1]
    ssm_w = y.shape[1]
    attn_w = attn.shape[1]
    row = lambda i: (i, 0)
    return pl.pallas_call(
        _out_ffn_kernel,
        grid=(T // tm,),
        in_specs=[
            pl.BlockSpec((tm, D), row), pl.BlockSpec((tm, ssm_w), row), pl.BlockSpec((tm, attn_w), row),
            _const_spec((1, D)), _const_spec((D, 2 * D)), _const_spec((ssm_w, ssm_w)), _const_spec((1, ssm_w)),
            _const_spec((ssm_w, D)), _const_spec((attn_w, D)), _const_spec((D, D)),
            _const_spec((1, D)), _const_spec((D, F)), _const_spec((D, F)), _const_spec((F, D)),
            _const_spec((1, D)),
        ],
        out_specs=pl.BlockSpec((tm, D), row),
        out_shape=jax.ShapeDtypeStruct((T, D), F32),
        compiler_params=pltpu.CompilerParams(
            dimension_semantics=("arbitrary",), vmem_limit_bytes=VMEM_LIMIT_BYTES),
        name="out_ffn2",
    )(x1, y, attn, gm, wgate, gluw, glub, wbs, wba, wout, g3, wg, wu, wd, gf)


def _rope_tables(seq):
    pos = jnp.arange(seq, dtype=F32)
    inv_freq = ROPE_THETA ** (-jnp.arange(0, HEAD_DIM, 2, dtype=F32) / HEAD_DIM)
    ang = pos[:, None] * inv_freq[None, :]
    cos = jnp.cos(ang)
    sin = jnp.sin(ang)
    cos_t = jnp.tile(cos, (1, 2 * LANES // HEAD_DIM))
    sin_t = jnp.tile(jnp.concatenate([-sin, sin], axis=1), (1, LANES // HEAD_DIM))
    return cos_t, sin_t


TM_FFN = 512
TM_S5 = 4096


def kernel(x, ffn1_norm, ffn1_w_gate, ffn1_w_up, ffn1_w_down, mix_norm, w_in, ssm_a_re, ssm_a_im, ssm_b_re, ssm_b_im, ssm_c_re, ssm_c_im, ssm_d, ssm_log_dt, glu_w, glu_b, w_branch_ssm, w_branch_attn, w_out, ffn2_norm, ffn2_w_gate, ffn2_w_up, ffn2_w_down, final_norm):
    B, L, D = x.shape
    ssm_w = glu_w.shape[1]
    attn_w = w_branch_attn.shape[1]
    T = B * L
    assert ffn1_norm.shape[0] == 1, "single-layer trunk only"
    assert attn_w == N_HEADS * HEAD_DIM and L % TM_S5 == 0 and T % TM_FFN == 0
    assert TM_FFN % MOBA_BLOCK == 0 and L % TM_FFN == 0
    bf = lambda w: w[0].astype(BF16)
    row = lambda v: v[0][None]
    cos_t, sin_t = _rope_tables(L)
    nseg = TM_S5 // (SUBLANES * CHUNK)
    wi = w_in[0]
    off_v = ssm_w + 2 * attn_w
    off_g = ssm_w + 3 * attn_w
    x1, u, q, k, vt, kmean8 = _ffn_inproj(
        x.reshape(T, D), row(ffn1_norm), bf(ffn1_w_gate), bf(ffn1_w_up), bf(ffn1_w_down), row(mix_norm),
        wi[:, :off_v].astype(BF16), wi[:, off_v:off_g].T.astype(BF16), cos_t, sin_t, seq=L, tm=TM_FFN)
    w1, wc, a8, pw = _s5_tables(ssm_a_re[0], ssm_a_im[0], ssm_b_re[0], ssm_b_im[0],
                                ssm_c_re[0], ssm_c_im[0], ssm_log_dt[0], nseg)
    y = _s5(u, w1, wc, a8, pw, row(ssm_d), batch=B, seq=L, tm=TM_S5)
    kmean = kmean8[::SUBLANES].reshape(B, L // MOBA_BLOCK, attn_w)
    attn = _moba(q, k, vt, kmean, batch=B, seq=L)
    out = _out_ffn(
        x1, y, attn, row(mix_norm), wi[:, off_g:].astype(BF16), bf(glu_w), row(glu_b), bf(w_branch_ssm),
        bf(w_branch_attn), bf(w_out), row(ffn2_norm), bf(ffn2_w_gate), bf(ffn2_w_up), bf(ffn2_w_down),
        final_norm[None], tm=TM_FFN)
    return out.reshape(B, L, D)
```

```python
import functools
import math

import jax
import jax.numpy as jnp
from jax import lax
from jax.experimental import pallas as pl
from jax.experimental.pallas import tpu as pltpu

F32 = jnp.float32
BF16 = jnp.bfloat16

N_HEADS = 8
HEAD_DIM = 64
MOBA_BLOCK = 256
MOBA_TOPK = 3
SSM_GROUP = 16
SSM_STATE = 64
ROPE_THETA = 10000.0
RMS_EPS = 1e-6
NEG_INF = -1e30

LANES = 128
SUBLANES = 8
VMEM_LIMIT_BYTES = 56 * 1024 * 1024

CHUNK = 8
OCTET = LANES // SSM_GROUP
OCT_STATE = OCTET * SSM_STATE

VT_ROWS = HEAD_DIM + 16


def _rmsnorm(x, gain):
    inv = lax.rsqrt(jnp.mean(x * x, axis=-1, keepdims=True) + RMS_EPS)
    return (x * inv) * gain


def _dot(a, b):
    return jnp.dot(a, b, preferred_element_type=F32)


def _const_spec(shape):
    nd = len(shape)
    return pl.BlockSpec(shape, lambda *_: (0,) * nd, pipeline_mode=pl.Buffered(1))


def _rotary_tile(x, cos, sin_signed, first_half):
    swapped = jnp.where(first_half, pltpu.roll(x, LANES - HEAD_DIM // 2, 1), pltpu.roll(x, HEAD_DIM // 2, 1))
    return x * cos + swapped * sin_signed


def _ffn_inproj_kernel(x_ref, g1_ref, wg_ref, wu_ref, wd_ref, g2_ref, win_ref, wvt_ref, cos_ref, sin_ref, *rest,
                       attn_w, ssm_w, nblocks, cast_slabs):
    n_cast = len(cast_slabs)
    cast_in = rest[:n_cast]
    x1_ref, u_ref, q_ref, k_ref, vt_ref, kmean_ref = rest[n_cast:n_cast + 6]
    cast_out = rest[n_cast + 6:]
    for w_ref, wb_ref, n in zip(cast_in, cast_out, cast_slabs):
        @pl.when(pl.program_id(0) < n)
        def _():
            wb_ref[...] = w_ref[...].astype(BF16)

    x = x_ref[...]
    h = _rmsnorm(x, g1_ref[...]).astype(BF16)
    act = (jax.nn.silu(_dot(h, wg_ref[...])) * _dot(h, wu_ref[...])).astype(BF16)
    x1 = x + 0.5 * _dot(act, wd_ref[...])
    x1_ref[...] = x1

    h2 = _rmsnorm(x1, g2_ref[...]).astype(BF16)
    proj = _dot(h2, win_ref[...])
    u_ref[...] = proj[:, :ssm_w]

    cos = cos_ref[...]
    sin_signed = sin_ref[...]
    lane = lax.broadcasted_iota(jnp.int32, cos.shape, 1)
    first_half = (lane % HEAD_DIM) < (HEAD_DIM // 2)
    scale = HEAD_DIM ** -0.5 * math.log2(math.e)
    tm = x.shape[0]
    nblk = tm // MOBA_BLOCK
    key_blk = (pl.program_id(0) * nblk + lax.broadcasted_iota(jnp.int32, cos.shape, 0) // MOBA_BLOCK) % nblocks
    blk_onehot = (lane == key_blk).astype(BF16)
    for t in range(attn_w // LANES):
        qs = proj[:, ssm_w + t * LANES: ssm_w + (t + 1) * LANES]
        ks = proj[:, ssm_w + attn_w + t * LANES: ssm_w + attn_w + (t + 1) * LANES]
        q_ref[:, t * LANES:(t + 1) * LANES] = (_rotary_tile(qs, cos, sin_signed, first_half) * scale).astype(BF16)
        kr = _rotary_tile(ks, cos, sin_signed, first_half)
        k_ref[:, 2 * t * LANES:(2 * t + 1) * LANES] = kr.astype(BF16)
        k_ref[:, (2 * t + 1) * LANES:(2 * t + 2) * LANES] = blk_onehot
        for j in range(nblk):
            mean = jnp.mean(kr[j * MOBA_BLOCK:(j + 1) * MOBA_BLOCK], axis=0, keepdims=True)
            kmean_ref[j * SUBLANES:(j + 1) * SUBLANES, t * LANES:(t + 1) * LANES] = jnp.broadcast_to(
                mean, (SUBLANES, LANES))

    vt = lax.dot_general(wvt_ref[...], h2, (((1,), (1,)), ((), ())), preferred_element_type=F32).astype(BF16)
    ones = jnp.ones((VT_ROWS - HEAD_DIM, MOBA_BLOCK), BF16)
    for j in range(nblk):
        for hd in range(attn_w // HEAD_DIM):
            vt_ref[j, hd * VT_ROWS:hd * VT_ROWS + HEAD_DIM, :] = (
                vt[hd * HEAD_DIM:(hd + 1) * HEAD_DIM, j * MOBA_BLOCK:(j + 1) * MOBA_BLOCK])
            vt_ref[j, hd * VT_ROWS + HEAD_DIM:(hd + 1) * VT_ROWS, :] = ones


BF16_SUBLANES = 16


def _cast_slabs(rows, steps):
    for n in range(steps, 0, -1):
        if rows % n == 0 and (rows // n) % BF16_SUBLANES == 0:
            return n
    raise ValueError(f"cannot slab {rows} rows over {steps} steps")


def _ffn_inproj(x2d, g1, wg, wu, wd, g2, win_uqk, wv_t, cos_t, sin_t, later_weights, *, seq, tm):
    T, D = x2d.shape
    F = wg.shape[1]
    attn_w = wv_t.shape[0]
    ssm_w = win_uqk.shape[1] - 2 * attn_w
    nblk = tm // MOBA_BLOCK
    tiles_per_seq = seq // tm
    steps = T // tm
    row = lambda i: (i, 0)
    cast_in, cast_out, cast_shapes, cast_slabs = [], [], [], []
    for w, cols in later_weights:
        n = _cast_slabs(w.shape[0], steps)
        width, cblk = cols if cols is not None else (w.shape[1], 0)
        cast_in.append(pl.BlockSpec((w.shape[0] // n, width),
                                    lambda i, n=n, cblk=cblk: (jnp.minimum(i, n - 1), cblk)))
        cast_out.append(pl.BlockSpec((w.shape[0] // n, width), lambda i, n=n: (jnp.minimum(i, n - 1), 0)))
        cast_shapes.append(jax.ShapeDtypeStruct((w.shape[0], width), BF16))
        cast_slabs.append(n)
    return pl.pallas_call(
        functools.partial(_ffn_inproj_kernel, attn_w=attn_w, ssm_w=ssm_w, nblocks=seq // MOBA_BLOCK,
                          cast_slabs=tuple(cast_slabs)),
        grid=(steps,),
        in_specs=[
            pl.BlockSpec((tm, D), row),
            _const_spec((1, D)), _const_spec((D, F)), _const_spec((D, F)), _const_spec((F, D)),
            _const_spec((1, D)), _const_spec((D, ssm_w + 2 * attn_w)), _const_spec((attn_w, D)),
            pl.BlockSpec((tm, LANES), lambda i: (i % tiles_per_seq, 0)),
            pl.BlockSpec((tm, LANES), lambda i: (i % tiles_per_seq, 0)),
        ] + cast_in,
        out_specs=[
            pl.BlockSpec((tm, D), row),
            pl.BlockSpec((tm, ssm_w), row),
            pl.BlockSpec((tm, attn_w), row),
            pl.BlockSpec((tm, 2 * attn_w), row),
            pl.BlockSpec((nblk, attn_w // HEAD_DIM * VT_ROWS, MOBA_BLOCK), lambda i: (i, 0, 0)),
            pl.BlockSpec((nblk * SUBLANES, attn_w), row),
        ] + cast_out,
        out_shape=[
            jax.ShapeDtypeStruct((T, D), F32),
            jax.ShapeDtypeStruct((T, ssm_w), F32),
            jax.ShapeDtypeStruct((T, attn_w), BF16),
            jax.ShapeDtypeStruct((T, 2 * attn_w), BF16),
            jax.ShapeDtypeStruct((T // MOBA_BLOCK, attn_w // HEAD_DIM * VT_ROWS, MOBA_BLOCK), BF16),
            jax.ShapeDtypeStruct((T // MOBA_BLOCK * SUBLANES, attn_w), F32),
        ] + cast_shapes,
        compiler_params=pltpu.CompilerParams(
            dimension_semantics=("arbitrary",), vmem_limit_bytes=VMEM_LIMIT_BYTES),
        name="ffn1_inproj",
    )(x2d, g1, wg, wu, wd, g2, win_uqk, wv_t, cos_t, sin_t, *[w for w, _ in later_weights])


def _s5_kernel(u_ref, w1_ref, wc_ref, a8_ref, pw_ref, d_ref, y_ref,
               lhs_scr, z_scr, y_scr, e_scr, ein_scr, carry_scr, *, tm, nseg):
    S = OCT_STATE
    CW = CHUNK * LANES
    seg_stride = tm // SUBLANES

    @pl.when(pl.program_id(2) == 0)
    def _():
        carry_scr[...] = jnp.zeros_like(carry_scr)

    for j in range(nseg):
        for i in range(CHUNK):
            lhs_scr[j * SUBLANES:(j + 1) * SUBLANES, i * LANES:(i + 1) * LANES] = (
                u_ref[pl.ds(CHUNK * j + i, SUBLANES, stride=seg_stride), :])

    lhs = lhs_scr[...].astype(BF16)
    z_scr[...] = _dot(lhs, w1_ref[0, :, CW:])
    y_scr[...] = _dot(lhs, w1_ref[0, :, :CW])

    a8 = a8_ref[0]
    ar = jnp.broadcast_to(a8[:, :S], (SUBLANES, S))
    ai = jnp.broadcast_to(a8[:, S:], (SUBLANES, S))
    er = jnp.zeros((SUBLANES, S), F32)
    ei = jnp.zeros((SUBLANES, S), F32)
    for j in range(nseg):
        rows = slice(j * SUBLANES, (j + 1) * SUBLANES)
        e_scr[rows, :S] = er
        e_scr[rows, S:] = ei
        zr = z_scr[rows, :S]
        zi = z_scr[rows, S:]
        er, ei = ar * er - ai * ei + zr, ar * ei + ai * er + zi
    ein_scr[:, :S] = er
    ein_scr[:, S:] = ei

    pw = pw_ref[0]
    pnr = pw[nseg:nseg + 1, :S]
    pni = pw[nseg:nseg + 1, S:]
    cr = carry_scr[0:1, :S]
    ci = carry_scr[0:1, S:]
    for s in range(SUBLANES):
        fr = ein_scr[s:s + 1, :S]
        fi = ein_scr[s:s + 1, S:]
        ein_scr[s:s + 1, :S] = cr
        ein_scr[s:s + 1, S:] = ci
        cr, ci = fr + pnr * cr - pni * ci, fi + pnr * ci + pni * cr
    carry_scr[0:1, :S] = cr
    carry_scr[0:1, S:] = ci

    einr = ein_scr[:, :S]
    eini = ein_scr[:, S:]
    for j in range(nseg):
        rows = slice(j * SUBLANES, (j + 1) * SUBLANES)
        pr = pw[j:j + 1, :S]
        pi = pw[j:j + 1, S:]
        e_scr[rows, :S] = e_scr[rows, :S] + (pr * einr - pi * eini)
        e_scr[rows, S:] = e_scr[rows, S:] + (pr * eini + pi * einr)

    y_scr[...] = y_scr[...] + _dot(e_scr[...].astype(BF16), wc_ref[0])

    d = d_ref[...]
    for j in range(nseg):
        rows = slice(j * SUBLANES, (j + 1) * SUBLANES)
        for i in range(CHUNK):
            cols = slice(i * LANES, (i + 1) * LANES)
            y_ref[pl.ds(CHUNK * j + i, SUBLANES, stride=seg_stride), :] = (
                y_scr[rows, cols] + d * lhs_scr[rows, cols])


def _s5(u, w1, wc, a8, pw, d, *, batch, seq, tm):
    T, ssm_w = u.shape
    nseg = tm // (SUBLANES * CHUNK)
    nc = tm // CHUNK
    CW = CHUNK * LANES
    S2 = 2 * OCT_STATE
    n_oct = ssm_w // LANES
    tiles = seq // tm
    return pl.pallas_call(
        functools.partial(_s5_kernel, tm=tm, nseg=nseg),
        grid=(n_oct, batch, tiles),
        in_specs=[
            pl.BlockSpec((tm, LANES), lambda o, b, t: (b * tiles + t, o)),
            pl.BlockSpec((1, CW, CW + S2), lambda o, b, t: (o, 0, 0)),
            pl.BlockSpec((1, S2, CW), lambda o, b, t: (o, 0, 0)),
            pl.BlockSpec((1, 1, S2), lambda o, b, t: (o, 0, 0)),
            pl.BlockSpec((1, nseg + 1, S2), lambda o, b, t: (o, 0, 0)),
            pl.BlockSpec((1, LANES), lambda o, b, t: (0, o)),
        ],
        out_specs=pl.BlockSpec((tm, LANES), lambda o, b, t: (b * tiles + t, o)),
        out_shape=jax.ShapeDtypeStruct((T, ssm_w), F32),
        scratch_shapes=[
            pltpu.VMEM((nc, CW), F32),
            pltpu.VMEM((nc, S2), F32),
            pltpu.VMEM((nc, CW), F32),
            pltpu.VMEM((nc, S2), F32),
            pltpu.VMEM((SUBLANES, S2), F32),
            pltpu.VMEM((SUBLANES, S2), F32),
        ],
        compiler_params=pltpu.CompilerParams(
            dimension_semantics=("arbitrary", "arbitrary", "arbitrary"), vmem_limit_bytes=VMEM_LIMIT_BYTES),
        name="s5_scan",
    )(u, w1, wc, a8, pw, d)


def _s5_tables(a_re, a_im, b_re, b_im, c_re, c_im, log_dt, nseg):
    G, P = a_re.shape
    H = b_re.shape[-1]
    n_oct = G // OCTET
    dt = jnp.exp(log_dt)[:, None]
    lam_r = a_re * dt
    lam_i = a_im * dt

    def powers(n):
        n = n.astype(F32)[:, None, None]
        mag = jnp.exp(lam_r * n)
        return mag * jnp.cos(lam_i * n), mag * jnp.sin(lam_i * n)

    pr, pi = powers(jnp.arange(CHUNK + 1))
    den = a_re * a_re + a_im * a_im
    nr = pr[1] - 1.0
    ni = pi[1]
    fr = ((nr * a_re + ni * a_im) / den)[..., None]
    fi = ((ni * a_re - nr * a_im) / den)[..., None]
    bbr = fr * b_re - fi * b_im
    bbi = fr * b_im + fi * b_re

    def octet_rows(x):
        lead = x.shape[:-3]
        x = x.reshape(lead + (n_oct, OCTET * H, P))
        x = jnp.concatenate([x, x], axis=-1)
        return jnp.moveaxis(x, len(lead), 0)

    bt = octet_rows(jnp.stack([bbr, bbi]).transpose(0, 1, 3, 2))
    ct = octet_rows(jnp.stack([c_re, c_im]))
    pwr = jnp.broadcast_to(jnp.stack([pr, pi], axis=1)[:, :, :, None, :], (CHUNK + 1, 2, G, H, P))
    pwr = octet_rows(pwr).reshape(n_oct, 2 * (CHUNK + 1), LANES, LANES)
    w1, w_c = _s5_table_call(pwr, bt, ct)

    a8 = jnp.concatenate([pr[CHUNK].reshape(n_oct, 1, OCT_STATE), pi[CHUNK].reshape(n_oct, 1, OCT_STATE)], axis=2)
    qr, qi = powers(CHUNK * jnp.arange(nseg + 1))
    pw = jnp.concatenate([qr.reshape(nseg + 1, n_oct, OCT_STATE), qi.reshape(nseg + 1, n_oct, OCT_STATE)], axis=2)
    return w1, w_c, a8, pw.transpose(1, 0, 2)


def _s5_table_kernel(pw_ref, bt_ref, ct_ref, w1_ref, wc_ref):
    S = OCT_STATE
    CW = CHUNK * LANES
    row = lax.broadcasted_iota(jnp.int32, (LANES, LANES), 0)
    lane = lax.broadcasted_iota(jnp.int32, (LANES, LANES), 1)
    same_group = (row // SSM_GROUP) == (lane // SSM_GROUP)
    first_copy = lane < SSM_STATE
    wide_row = lax.broadcasted_iota(jnp.int32, (LANES, S), 0)
    wide_lane = lax.broadcasted_iota(jnp.int32, (LANES, S), 1)
    own_states = (wide_row // SSM_GROUP) == (wide_lane // SSM_STATE)

    def spread(x):
        return jnp.where(own_states, jnp.concatenate([x] * (S // LANES), axis=1), 0.0)

    def nt_dot(a, b):
        return lax.dot_general(a, b, (((1,), (1,)), ((), ())), precision=lax.Precision.HIGHEST,
                               preferred_element_type=F32)

    btr, bti = bt_ref[0, 0], bt_ref[0, 1]
    ctr, cti = ct_ref[0, 0], ct_ref[0, 1]
    ctr_once = jnp.where(first_copy, ctr, 0.0)
    cti_once = jnp.where(first_copy, cti, 0.0)
    zero_tile = jnp.zeros((LANES, LANES), BF16)
    for n in range(CHUNK):
        pr, pi = pw_ref[0, 2 * n], pw_ref[0, 2 * n + 1]
        xr = btr * pr - bti * pi
        xi = btr * pi + bti * pr
        kern = jnp.where(same_group, nt_dot(xr, ctr_once) - nt_dot(xi, cti_once), 0.0).astype(BF16)
        for i in range(CHUNK - n):
            w1_ref[0, i * LANES:(i + 1) * LANES, (i + n) * LANES:(i + n + 1) * LANES] = kern
        for i in range(n, CHUNK) if n else ():
            w1_ref[0, i * LANES:(i + 1) * LANES, (i - n) * LANES:(i - n + 1) * LANES] = zero_tile
        i = CHUNK - 1 - n
        w1_ref[0, i * LANES:(i + 1) * LANES, CW:CW + S] = spread(xr).astype(BF16)
        w1_ref[0, i * LANES:(i + 1) * LANES, CW + S:] = spread(xi).astype(BF16)
    for j in range(CHUNK):
        pr, pi = pw_ref[0, 2 * (j + 1)], pw_ref[0, 2 * (j + 1) + 1]
        wc_ref[0, :S, j * LANES:(j + 1) * LANES] = spread(ctr * pr - cti * pi).T.astype(BF16)
        wc_ref[0, S:, j * LANES:(j + 1) * LANES] = spread(-(ctr * pi + cti * pr)).T.astype(BF16)


def _s5_table_call(pwr, bt, ct):
    n_oct = pwr.shape[0]
    CW = CHUNK * LANES
    S2 = 2 * OCT_STATE
    tile4 = lambda n: pl.BlockSpec((1, n, LANES, LANES), lambda o: (o, 0, 0, 0))
    return pl.pallas_call(
        _s5_table_kernel,
        grid=(n_oct,),
        in_specs=[tile4(pwr.shape[1]), tile4(2), tile4(2)],
        out_specs=[pl.BlockSpec((1, CW, CW + S2), lambda o: (o, 0, 0)),
                   pl.BlockSpec((1, S2, CW), lambda o: (o, 0, 0))],
        out_shape=[jax.ShapeDtypeStruct((n_oct, CW, CW + S2), BF16),
                   jax.ShapeDtypeStruct((n_oct, S2, CW), BF16)],
        compiler_params=pltpu.CompilerParams(
            dimension_semantics=("arbitrary",), vmem_limit_bytes=VMEM_LIMIT_BYTES),
        name="s5_tables",
    )(pwr, bt, ct)


KV_BLOCKS_PER_STEP = 2

def _moba_kernel(q_ref, k_ref, vt_ref, km_ref, o_ref,
                 qt_scr, s0_scr, s1_scr, smax0_scr, smax1_scr, m_scr, acc_scr, *, nblocks):
    t = pl.program_id(2)
    BK = MOBA_BLOCK
    NB = KV_BLOCKS_PER_STEP
    QT = q_ref.shape[0]

    q2t = q_ref[...].astype(F32).T.astype(BF16)
    km = km_ref[0]
    km_hi = km.astype(BF16)
    km_lo = (km - km_hi.astype(F32)).astype(BF16)
    blk = lax.broadcasted_iota(jnp.int32, (nblocks, QT), 0)
    blk_f = blk.astype(F32)
    own = t * (QT // BK) + lax.broadcasted_iota(jnp.int32, (nblocks, QT), 1) // BK
    pad = jnp.zeros((LANES - nblocks, QT), BF16)
    no_feat = jnp.zeros((HEAD_DIM, QT), BF16)
    for a in range(2):
        head = q2t[a * HEAD_DIM:(a + 1) * HEAD_DIM]
        qat = jnp.concatenate([head, no_feat] if a == 0 else [no_feat, head], axis=0)
        gate = _dot(km_hi, qat) + _dot(km_lo, qat)
        gate = jnp.where(blk < own, gate, NEG_INF)
        sel = jnp.zeros(gate.shape, jnp.bool_)
        for _ in range(MOBA_TOPK):
            top = jnp.max(gate, axis=0, keepdims=True)
            idx = jnp.min(jnp.where(gate == top, blk_f, float(nblocks)), axis=0, keepdims=True)
            pick = blk_f == idx
            sel = sel | pick
            gate = jnp.where(pick, -jnp.inf, gate)
        bias = jnp.where((sel & (blk < own)) | (blk == own), 0.0, NEG_INF).astype(BF16)
        qt_scr[a] = jnp.concatenate([qat, bias, pad], axis=0)
        m_scr[a] = jnp.full((1, QT), NEG_INF, F32)
        acc_scr[a] = jnp.zeros((VT_ROWS, QT), F32)

    all_q = slice(0, QT)
    late_q = slice(NB * BK, QT)

    def scores(c, s_scr, smax_scr, qs=all_q):
        kb = k_ref[pl.ds(pl.multiple_of(c * (NB * BK), NB * BK), NB * BK), :]
        for a in range(2):
            s = _dot(kb, qt_scr[a, :, qs])
            s_scr[a, :, qs] = s
            smax_scr[a, :, qs] = jnp.max(s, axis=0, keepdims=True)

    def consume(c, s_scr, smax_scr, qs=all_q):
        for a in range(2):
            vta = jnp.concatenate(
                [vt_ref[c * NB + j, a * VT_ROWS:(a + 1) * VT_ROWS, :] for j in range(NB)], axis=1)
            m_old = m_scr[a, :, qs]
            m_new = jnp.maximum(m_old, smax_scr[a, :, qs])
            p = jnp.exp2(s_scr[a, :, qs] - m_new).astype(BF16)
            acc_scr[a, :, qs] = jnp.exp2(m_old - m_new) * acc_scr[a, :, qs] + _dot(vta, p)
            m_scr[a, :, qs] = m_new

    tri = (lax.broadcasted_iota(jnp.int32, (BK, BK), 0) <= lax.broadcasted_iota(jnp.int32, (BK, BK), 1))

    def causal_patch(e, s_scr, smax_scr, qs=all_q):
        for a in range(2):
            for j in range(NB):
                rows = slice(j * BK, (j + 1) * BK)
                cols = slice((e * NB + j) * BK, (e * NB + j + 1) * BK)
                s_scr[a, rows, cols] = jnp.where(tri, s_scr[a, rows, cols], NEG_INF)
            smax_scr[a, :, qs] = jnp.max(s_scr[a, :, qs], axis=0, keepdims=True)

    buf0 = (s0_scr, smax0_scr)
    buf1 = (s1_scr, smax1_scr)
    scores(0, *buf0)

    def body(i, carry):
        scores(2 * i + 1, *buf1)
        consume(2 * i, *buf0)
        scores(2 * i + 2, *buf0)
        consume(2 * i + 1, *buf1)
        return carry

    lax.fori_loop(0, t, body, 0)
    scores(2 * t + 1, *buf1, qs=late_q)
    causal_patch(0, *buf0)
    consume(2 * t, *buf0)
    causal_patch(1, *buf1, qs=late_q)
    consume(2 * t + 1, *buf1, qs=late_q)
    ot = jnp.concatenate([acc_scr[a, :HEAD_DIM] / acc_scr[a, HEAD_DIM:HEAD_DIM + 1] for a in range(2)], axis=0)
    o_ref[...] = ot.T.astype(BF16)


def _moba(q, k, vt, kmean, *, batch, seq):
    T, attn_w = q.shape
    nblocks = seq // MOBA_BLOCK
    n_pairs = attn_w // LANES
    BK = MOBA_BLOCK
    QT = 2 * KV_BLOCKS_PER_STEP * BK
    tiles = seq // QT
    return pl.pallas_call(
        functools.partial(_moba_kernel, nblocks=nblocks),
        grid=(batch, n_pairs, tiles),
        in_specs=[
            pl.BlockSpec((QT, LANES), lambda b, h, i: (b * tiles + i, h)),
            pl.BlockSpec((seq, 2 * LANES), lambda b, h, i: (b, h)),
            pl.BlockSpec((nblocks, 2 * VT_ROWS, BK), lambda b, h, i: (b, h, 0)),
            pl.BlockSpec((1, nblocks, LANES), lambda b, h, i: (b, 0, h)),
        ],
        out_specs=pl.BlockSpec((QT, LANES), lambda b, h, i: (b * tiles + i, h)),
        out_shape=jax.ShapeDtypeStruct((T, attn_w), BF16),
        scratch_shapes=[
            pltpu.VMEM((2, 2 * LANES, QT), BF16),
            pltpu.VMEM((2, KV_BLOCKS_PER_STEP * BK, QT), F32),
            pltpu.VMEM((2, KV_BLOCKS_PER_STEP * BK, QT), F32),
            pltpu.VMEM((2, 1, QT), F32),
            pltpu.VMEM((2, 1, QT), F32),
            pltpu.VMEM((2, 1, QT), F32),
            pltpu.VMEM((2, VT_ROWS, QT), F32),
        ],
        compiler_params=pltpu.CompilerParams(
            dimension_semantics=("arbitrary", "arbitrary", "arbitrary"), vmem_limit_bytes=VMEM_LIMIT_BYTES),
        name="moba_attn",
    )(q, k, vt, kmean)


def _out_ffn_kernel(x1_ref, y_ref, a_ref, gm_ref, wgate_ref, gluw_ref, glub_ref, wbs_ref, wba_ref, wout_ref,
                    g3_ref, wg_ref, wu_ref, wd_ref, gf_ref, o_ref):
    x1 = x1_ref[...]
    D = x1.shape[1]
    h = _rmsnorm(x1, gm_ref[...]).astype(BF16)
    gates = _dot(h, wgate_ref[...])
    ys = jax.nn.gelu(y_ref[...])
    ys = ys * jax.nn.sigmoid(_dot(ys.astype(BF16), gluw_ref[...]) + glub_ref[...])
    branch_a = _dot(ys.astype(BF16), wbs_ref[...])
    branch_b = _dot(a_ref[...], wba_ref[...])
    merged = jax.nn.sigmoid(gates[:, :D]) * branch_a + jax.nn.sigmoid(gates[:, D:]) * branch_b
    x2 = x1 + _dot(merged.astype(BF16), wout_ref[...])

    h3 = _rmsnorm(x2, g3_ref[...]).astype(BF16)
    act = (jax.nn.silu(_dot(h3, wg_ref[...])) * _dot(h3, wu_ref[...])).astype(BF16)
    x3 = x2 + 0.5 * _dot(act, wd_ref[...])
    o_ref[...] = _rmsnorm(x3, gf_ref[...])


def _out_ffn(x1, y, attn, gm, wgate, gluw, glub, wbs, wba, wout, g3, wg, wu, wd, gf, *, tm):
    T, D = x1.shape
    F = wg.shape[1]
    ssm_w = y.shape[1]
    attn_w = attn.shape[1]
    row = lambda i: (i, 0)
    return pl.pallas_call(
        _out_ffn_kernel,
        grid=(T // tm,),
        in_specs=[
            pl.BlockSpec((tm, D), row), pl.BlockSpec((tm, ssm_w), row), pl.BlockSpec((tm, attn_w), row),
            _const_spec((1, D)), _const_spec((D, 2 * D)), _const_spec((ssm_w, ssm_w)), _const_spec((1, ssm_w)),
            _const_spec((ssm_w, D)), _const_spec((attn_w, D)), _const_spec((D, D)),
            _const_spec((1, D)), _const_spec((D, F)), _const_spec((D, F)), _const_spec((F, D)),
            _const_spec((1, D)),
        ],
        out_specs=pl.BlockSpec((tm, D), row),
        out_shape=jax.ShapeDtypeStruct((T, D), F32),
        compiler_params=pltpu.CompilerParams(
            dimension_semantics=("arbitrary",), vmem_limit_bytes=VMEM_LIMIT_BYTES),
        name="out_ffn2",
    )(x1, y, attn, gm, wgate, gluw, glub, wbs, wba, wout, g3, wg, wu, wd, gf)


def _rope_tables(seq):
    pos = jnp.arange(seq, dtype=F32)
    inv_freq = ROPE_THETA ** (-jnp.arange(0, HEAD_DIM, 2, dtype=F32) / HEAD_DIM)
    ang = pos[:, None] * inv_freq[None, :]
    cos = jnp.cos(ang)
    sin = jnp.sin(ang)
    cos_t = jnp.tile(cos, (1, 2 * LANES // HEAD_DIM))
    sin_t = jnp.tile(jnp.concatenate([-sin, sin], axis=1), (1, LANES // HEAD_DIM))
    return cos_t, sin_t


TM_FFN = 512
TM_S5 = 4096


def kernel(x, ffn1_norm, ffn1_w_gate, ffn1_w_up, ffn1_w_down, mix_norm, w_in, ssm_a_re, ssm_a_im, ssm_b_re, ssm_b_im, ssm_c_re, ssm_c_im, ssm_d, ssm_log_dt, glu_w, glu_b, w_branch_ssm, w_branch_attn, w_out, ffn2_norm, ffn2_w_gate, ffn2_w_up, ffn2_w_down, final_norm):
    B, L, D = x.shape
    ssm_w = glu_w.shape[1]
    attn_w = w_branch_attn.shape[1]
    T = B * L
    assert ffn1_norm.shape[0] == 1, "single-layer trunk only"
    assert attn_w == N_HEADS * HEAD_DIM and L % TM_S5 == 0 and T % TM_FFN == 0
    assert TM_FFN % MOBA_BLOCK == 0 and L % TM_FFN == 0
    bf = lambda w: w[0].astype(BF16)
    row = lambda v: v[0][None]
    cos_t, sin_t = _rope_tables(L)
    nseg = TM_S5 // (SUBLANES * CHUNK)
    wi = w_in[0]
    off_v = ssm_w + 2 * attn_w
    off_g = ssm_w + 3 * attn_w
    gate_w = wi.shape[1] - off_g
    assert off_g % gate_w == 0
    later = [(wi, (gate_w, off_g // gate_w)), (glu_w[0], None), (w_branch_ssm[0], None), (w_branch_attn[0], None),
             (w_out[0], None), (ffn2_w_gate[0], None), (ffn2_w_up[0], None), (ffn2_w_down[0], None)]
    x1, u, q, k, vt, kmean8, *later_bf16 = _ffn_inproj(
        x.reshape(T, D), row(ffn1_norm), bf(ffn1_w_gate), bf(ffn1_w_up), bf(ffn1_w_down), row(mix_norm),
        wi[:, :off_v].astype(BF16), wi[:, off_v:off_g].T.astype(BF16), cos_t, sin_t, later, seq=L, tm=TM_FFN)
    wgate, gluw, wbs, wba, wout, wg2, wu2, wd2 = later_bf16
    w1, wc, a8, pw = _s5_tables(ssm_a_re[0], ssm_a_im[0], ssm_b_re[0], ssm_b_im[0],
                                ssm_c_re[0], ssm_c_im[0], ssm_log_dt[0], nseg)
    y = _s5(u, w1, wc, a8, pw, row(ssm_d), batch=B, seq=L, tm=TM_S5)
    kmean = kmean8[::SUBLANES].reshape(B, L // MOBA_BLOCK, attn_w)
    attn = _moba(q, k, vt, kmean, batch=B, seq=L)
    out = _out_ffn(
        x1, y, attn, row(mix_norm), wgate, gluw, row(glu_b), wbs, wba, wout, row(ffn2_norm), wg2, wu2, wd2,
        final_norm[None], tm=TM_FFN)
    return out.reshape(B, L, D)
```

```python
import functools
import math

import jax
import jax.numpy as jnp
from jax import lax
from jax.experimental import pallas as pl
from jax.experimental.pallas import tpu as pltpu

F32 = jnp.float32
BF16 = jnp.bfloat16

N_HEADS = 8
HEAD_DIM = 64
MOBA_BLOCK = 256
MOBA_TOPK = 3
SSM_GROUP = 16
SSM_STATE = 64
ROPE_THETA = 10000.0
RMS_EPS = 1e-6
NEG_INF = -1e30

LANES = 128
SUBLANES = 8
VMEM_LIMIT_BYTES = 56 * 1024 * 1024

CHUNK = 8
OCTET = LANES // SSM_GROUP
OCT_STATE = OCTET * SSM_STATE

VT_ROWS = HEAD_DIM + 16

SEG_PAD = 8


def _rmsnorm(x, gain):
    inv = lax.rsqrt(jnp.mean(x * x, axis=-1, keepdims=True) + RMS_EPS)
    return (x * inv) * gain


def _dot(a, b):
    return jnp.dot(a, b, preferred_element_type=F32)


def _const_spec(shape):
    nd = len(shape)
    return pl.BlockSpec(shape, lambda *_: (0,) * nd, pipeline_mode=pl.Buffered(1))


def _rotary_tile(x, cos, sin_signed, first_half):
    swapped = jnp.where(first_half, pltpu.roll(x, LANES - HEAD_DIM // 2, 1), pltpu.roll(x, HEAD_DIM // 2, 1))
    return x * cos + swapped * sin_signed


def _ffn_inproj_kernel(x_ref, g1_ref, wg_ref, wu_ref, wd_ref, g2_ref, win_ref, wvt_ref, cos_ref, sin_ref, *rest,
                       attn_w, ssm_w, nblocks, cast_slabs):
    n_cast = len(cast_slabs)
    cast_in = rest[:n_cast]
    x1_ref, u_ref, q_ref, k_ref, vt_ref, kmean_ref = rest[n_cast:n_cast + 6]
    cast_out = rest[n_cast + 6:]
    for w_ref, wb_ref, n in zip(cast_in, cast_out, cast_slabs):
        @pl.when(pl.program_id(0) < n)
        def _():
            wb_ref[...] = w_ref[...].astype(BF16)

    x = x_ref[...]
    h = _rmsnorm(x, g1_ref[...]).astype(BF16)
    act = (jax.nn.silu(_dot(h, wg_ref[...])) * _dot(h, wu_ref[...])).astype(BF16)
    x1 = x + 0.5 * _dot(act, wd_ref[...])
    x1_ref[...] = x1

    h2 = _rmsnorm(x1, g2_ref[...]).astype(BF16)
    proj = _dot(h2, win_ref[...])
    tm = x.shape[0]
    u_ref[:tm, :] = proj[:, :ssm_w]
    u_ref[tm:, :] = jnp.zeros((SEG_PAD, ssm_w), F32)

    cos = cos_ref[...]
    sin_signed = sin_ref[...]
    lane = lax.broadcasted_iota(jnp.int32, cos.shape, 1)
    first_half = (lane % HEAD_DIM) < (HEAD_DIM // 2)
    scale = HEAD_DIM ** -0.5 * math.log2(math.e)
    nblk = tm // MOBA_BLOCK
    key_blk = (pl.program_id(0) * nblk + lax.broadcasted_iota(jnp.int32, cos.shape, 0) // MOBA_BLOCK) % nblocks
    blk_onehot = (lane == key_blk).astype(BF16)
    for t in range(attn_w // LANES):
        qs = proj[:, ssm_w + t * LANES: ssm_w + (t + 1) * LANES]
        ks = proj[:, ssm_w + attn_w + t * LANES: ssm_w + attn_w + (t + 1) * LANES]
        q_ref[:, t * LANES:(t + 1) * LANES] = (_rotary_tile(qs, cos, sin_signed, first_half) * scale).astype(BF16)
        kr = _rotary_tile(ks, cos, sin_signed, first_half)
        k_ref[:, 2 * t * LANES:(2 * t + 1) * LANES] = kr.astype(BF16)
        k_ref[:, (2 * t + 1) * LANES:(2 * t + 2) * LANES] = blk_onehot
        for j in range(nblk):
            mean = jnp.mean(kr[j * MOBA_BLOCK:(j + 1) * MOBA_BLOCK], axis=0, keepdims=True)
            kmean_ref[j * SUBLANES:(j + 1) * SUBLANES, t * LANES:(t + 1) * LANES] = jnp.broadcast_to(
                mean, (SUBLANES, LANES))

    vt = lax.dot_general(wvt_ref[...], h2, (((1,), (1,)), ((), ())), preferred_element_type=F32).astype(BF16)
    ones = jnp.ones((VT_ROWS - HEAD_DIM, MOBA_BLOCK), BF16)
    for j in range(nblk):
        for hd in range(attn_w // HEAD_DIM):
            vt_ref[j, hd * VT_ROWS:hd * VT_ROWS + HEAD_DIM, :] = (
                vt[hd * HEAD_DIM:(hd + 1) * HEAD_DIM, j * MOBA_BLOCK:(j + 1) * MOBA_BLOCK])
            vt_ref[j, hd * VT_ROWS + HEAD_DIM:(hd + 1) * VT_ROWS, :] = ones


BF16_SUBLANES = 16


def _cast_slabs(rows, steps):
    for n in range(steps, 0, -1):
        if rows % n == 0 and (rows // n) % BF16_SUBLANES == 0:
            return n
    raise ValueError(f"cannot slab {rows} rows over {steps} steps")


def _ffn_inproj(x2d, g1, wg, wu, wd, g2, win_uqk, wv_t, cos_t, sin_t, later_weights, *, seq, tm):
    T, D = x2d.shape
    F = wg.shape[1]
    attn_w = wv_t.shape[0]
    ssm_w = win_uqk.shape[1] - 2 * attn_w
    nblk = tm // MOBA_BLOCK
    tiles_per_seq = seq // tm
    steps = T // tm
    row = lambda i: (i, 0)
    cast_in, cast_out, cast_shapes, cast_slabs = [], [], [], []
    for w, cols in later_weights:
        n = _cast_slabs(w.shape[0], steps)
        width, cblk = cols if cols is not None else (w.shape[1], 0)
        cast_in.append(pl.BlockSpec((w.shape[0] // n, width),
                                    lambda i, n=n, cblk=cblk: (jnp.minimum(i, n - 1), cblk)))
        cast_out.append(pl.BlockSpec((w.shape[0] // n, width), lambda i, n=n: (jnp.minimum(i, n - 1), 0)))
        cast_shapes.append(jax.ShapeDtypeStruct((w.shape[0], width), BF16))
        cast_slabs.append(n)
    return pl.pallas_call(
        functools.partial(_ffn_inproj_kernel, attn_w=attn_w, ssm_w=ssm_w, nblocks=seq // MOBA_BLOCK,
                          cast_slabs=tuple(cast_slabs)),
        grid=(steps,),
        in_specs=[
            pl.BlockSpec((tm, D), row),
            _const_spec((1, D)), _const_spec((D, F)), _const_spec((D, F)), _const_spec((F, D)),
            _const_spec((1, D)), _const_spec((D, ssm_w + 2 * attn_w)), _const_spec((attn_w, D)),
            pl.BlockSpec((tm, LANES), lambda i: (i % tiles_per_seq, 0)),
            pl.BlockSpec((tm, LANES), lambda i: (i % tiles_per_seq, 0)),
        ] + cast_in,
        out_specs=[
            pl.BlockSpec((tm, D), row),
            pl.BlockSpec((tm + SEG_PAD, ssm_w), row),
            pl.BlockSpec((tm, attn_w), row),
            pl.BlockSpec((tm, 2 * attn_w), row),
            pl.BlockSpec((nblk, attn_w // HEAD_DIM * VT_ROWS, MOBA_BLOCK), lambda i: (i, 0, 0)),
            pl.BlockSpec((nblk * SUBLANES, attn_w), row),
        ] + cast_out,
        out_shape=[
            jax.ShapeDtypeStruct((T, D), F32),
            jax.ShapeDtypeStruct((steps * (tm + SEG_PAD), ssm_w), F32),
            jax.ShapeDtypeStruct((T, attn_w), BF16),
            jax.ShapeDtypeStruct((T, 2 * attn_w), BF16),
            jax.ShapeDtypeStruct((T // MOBA_BLOCK, attn_w // HEAD_DIM * VT_ROWS, MOBA_BLOCK), BF16),
            jax.ShapeDtypeStruct((T // MOBA_BLOCK * SUBLANES, attn_w), F32),
        ] + cast_shapes,
        compiler_params=pltpu.CompilerParams(
            dimension_semantics=("arbitrary",), vmem_limit_bytes=VMEM_LIMIT_BYTES),
        name="ffn1_inproj",
    )(x2d, g1, wg, wu, wd, g2, win_uqk, wv_t, cos_t, sin_t, *[w for w, _ in later_weights])


def _s5_kernel(u_ref, w1_ref, wc_ref, a8_ref, pw_ref, d_ref, y_ref,
               lhs_scr, z_scr, y_scr, e_scr, ein_scr, carry_scr, *, tm, nseg):
    S = OCT_STATE
    CW = CHUNK * LANES
    seg_stride = tm // SUBLANES + SEG_PAD

    @pl.when(pl.program_id(2) == 0)
    def _():
        carry_scr[...] = jnp.zeros_like(carry_scr)

    for j in range(nseg):
        for i in range(CHUNK):
            lhs_scr[j * SUBLANES:(j + 1) * SUBLANES, i * LANES:(i + 1) * LANES] = (
                u_ref[pl.ds(CHUNK * j + i, SUBLANES, stride=seg_stride), :])

    lhs = lhs_scr[...].astype(BF16)
    z_scr[...] = _dot(lhs, w1_ref[0, :, CW:])
    y_scr[...] = _dot(lhs, w1_ref[0, :, :CW])

    a8 = a8_ref[0]
    ar = jnp.broadcast_to(a8[:, :S], (SUBLANES, S))
    ai = jnp.broadcast_to(a8[:, S:], (SUBLANES, S))
    er = jnp.zeros((SUBLANES, S), F32)
    ei = jnp.zeros((SUBLANES, S), F32)
    for j in range(nseg):
        rows = slice(j * SUBLANES, (j + 1) * SUBLANES)
        e_scr[rows, :S] = er
        e_scr[rows, S:] = ei
        zr = z_scr[rows, :S]
        zi = z_scr[rows, S:]
        er, ei = ar * er - ai * ei + zr, ar * ei + ai * er + zi
    ein_scr[:, :S] = er
    ein_scr[:, S:] = ei

    pw = pw_ref[0]
    pnr = pw[nseg:nseg + 1, :S]
    pni = pw[nseg:nseg + 1, S:]
    cr = carry_scr[0:1, :S]
    ci = carry_scr[0:1, S:]
    for s in range(SUBLANES):
        fr = ein_scr[s:s + 1, :S]
        fi = ein_scr[s:s + 1, S:]
        ein_scr[s:s + 1, :S] = cr
        ein_scr[s:s + 1, S:] = ci
        cr, ci = fr + pnr * cr - pni * ci, fi + pnr * ci + pni * cr
    carry_scr[0:1, :S] = cr
    carry_scr[0:1, S:] = ci

    einr = ein_scr[:, :S]
    eini = ein_scr[:, S:]
    for j in range(nseg):
        rows = slice(j * SUBLANES, (j + 1) * SUBLANES)
        pr = pw[j:j + 1, :S]
        pi = pw[j:j + 1, S:]
        e_scr[rows, :S] = e_scr[rows, :S] + (pr * einr - pi * eini)
        e_scr[rows, S:] = e_scr[rows, S:] + (pr * eini + pi * einr)

    y_scr[...] = y_scr[...] + _dot(e_scr[...].astype(BF16), wc_ref[0])

    d = d_ref[...]
    for j in range(nseg):
        rows = slice(j * SUBLANES, (j + 1) * SUBLANES)
        for i in range(CHUNK):
            cols = slice(i * LANES, (i + 1) * LANES)
            y_ref[pl.ds(CHUNK * j + i, SUBLANES, stride=seg_stride), :] = (
                y_scr[rows, cols] + d * lhs_scr[rows, cols])
    for s in range(SUBLANES):
        y_ref[(s + 1) * seg_stride - SEG_PAD:(s + 1) * seg_stride, :] = jnp.zeros((SEG_PAD, LANES), F32)


def _s5(u, w1, wc, a8, pw, d, *, batch, seq, tm):
    ssm_w = u.shape[1]
    tm_rows = tm + SUBLANES * SEG_PAD
    nseg = tm // (SUBLANES * CHUNK)
    nc = tm // CHUNK
    CW = CHUNK * LANES
    S2 = 2 * OCT_STATE
    n_oct = ssm_w // LANES
    tiles = seq // tm
    return pl.pallas_call(
        functools.partial(_s5_kernel, tm=tm, nseg=nseg),
        grid=(n_oct, batch, tiles),
        in_specs=[
            pl.BlockSpec((tm_rows, LANES), lambda o, b, t: (b * tiles + t, o)),
            pl.BlockSpec((1, CW, CW + S2), lambda o, b, t: (o, 0, 0)),
            pl.BlockSpec((1, S2, CW), lambda o, b, t: (o, 0, 0)),
            pl.BlockSpec((1, 1, S2), lambda o, b, t: (o, 0, 0)),
            pl.BlockSpec((1, nseg + 1, S2), lambda o, b, t: (o, 0, 0)),
            pl.BlockSpec((1, LANES), lambda o, b, t: (0, o)),
        ],
        out_specs=pl.BlockSpec((tm_rows, LANES), lambda o, b, t: (b * tiles + t, o)),
        out_shape=jax.ShapeDtypeStruct(u.shape, F32),
        scratch_shapes=[
            pltpu.VMEM((nc, CW), F32),
            pltpu.VMEM((nc, S2), F32),
            pltpu.VMEM((nc, CW), F32),
            pltpu.VMEM((nc, S2), F32),
            pltpu.VMEM((SUBLANES, S2), F32),
            pltpu.VMEM((SUBLANES, S2), F32),
        ],
        compiler_params=pltpu.CompilerParams(
            dimension_semantics=("arbitrary", "arbitrary", "arbitrary"), vmem_limit_bytes=VMEM_LIMIT_BYTES),
        name="s5_scan",
    )(u, w1, wc, a8, pw, d)


def _s5_tables(a_re, a_im, b_re, b_im, c_re, c_im, log_dt, nseg):
    G, P = a_re.shape
    H = b_re.shape[-1]
    n_oct = G // OCTET
    dt = jnp.exp(log_dt)[:, None]
    lam_r = a_re * dt
    lam_i = a_im * dt

    def powers(n):
        n = n.astype(F32)[:, None, None]
        mag = jnp.exp(lam_r * n)
        return mag * jnp.cos(lam_i * n), mag * jnp.sin(lam_i * n)

    pr, pi = powers(jnp.arange(CHUNK + 1))
    den = a_re * a_re + a_im * a_im
    nr = pr[1] - 1.0
    ni = pi[1]
    fr = ((nr * a_re + ni * a_im) / den)[..., None]
    fi = ((ni * a_re - nr * a_im) / den)[..., None]
    bbr = fr * b_re - fi * b_im
    bbi = fr * b_im + fi * b_re

    def octet_rows(x):
        lead = x.shape[:-3]
        x = x.reshape(lead + (n_oct, OCTET * H, P))
        x = jnp.concatenate([x, x], axis=-1)
        return jnp.moveaxis(x, len(lead), 0)

    bt = octet_rows(jnp.stack([bbr, bbi]).transpose(0, 1, 3, 2))
    ct = octet_rows(jnp.stack([c_re, c_im]))
    pwr = jnp.broadcast_to(jnp.stack([pr, pi], axis=1)[:, :, :, None, :], (CHUNK + 1, 2, G, H, P))
    pwr = octet_rows(pwr).reshape(n_oct, 2 * (CHUNK + 1), LANES, LANES)
    w1, w_c = _s5_table_call(pwr, bt, ct)

    a8 = jnp.concatenate([pr[CHUNK].reshape(n_oct, 1, OCT_STATE), pi[CHUNK].reshape(n_oct, 1, OCT_STATE)], axis=2)
    qr, qi = powers(CHUNK * jnp.arange(nseg + 1))
    pw = jnp.concatenate([qr.reshape(nseg + 1, n_oct, OCT_STATE), qi.reshape(nseg + 1, n_oct, OCT_STATE)], axis=2)
    return w1, w_c, a8, pw.transpose(1, 0, 2)


def _s5_table_kernel(pw_ref, bt_ref, ct_ref, w1_ref, wc_ref):
    S = OCT_STATE
    CW = CHUNK * LANES
    row = lax.broadcasted_iota(jnp.int32, (LANES, LANES), 0)
    lane = lax.broadcasted_iota(jnp.int32, (LANES, LANES), 1)
    same_group = (row // SSM_GROUP) == (lane // SSM_GROUP)
    first_copy = lane < SSM_STATE
    wide_row = lax.broadcasted_iota(jnp.int32, (LANES, S), 0)
    wide_lane = lax.broadcasted_iota(jnp.int32, (LANES, S), 1)
    own_states = (wide_row // SSM_GROUP) == (wide_lane // SSM_STATE)

    def spread(x):
        return jnp.where(own_states, jnp.concatenate([x] * (S // LANES), axis=1), 0.0)

    def nt_dot(a, b):
        return lax.dot_general(a, b, (((1,), (1,)), ((), ())), precision=lax.Precision.HIGHEST,
                               preferred_element_type=F32)

    btr, bti = bt_ref[0, 0], bt_ref[0, 1]
    ctr, cti = ct_ref[0, 0], ct_ref[0, 1]
    ctr_once = jnp.where(first_copy, ctr, 0.0)
    cti_once = jnp.where(first_copy, cti, 0.0)
    zero_tile = jnp.zeros((LANES, LANES), BF16)
    for n in range(CHUNK):
        pr, pi = pw_ref[0, 2 * n], pw_ref[0, 2 * n + 1]
        xr = btr * pr - bti * pi
        xi = btr * pi + bti * pr
        kern = jnp.where(same_group, nt_dot(xr, ctr_once) - nt_dot(xi, cti_once), 0.0).astype(BF16)
        for i in range(CHUNK - n):
            w1_ref[0, i * LANES:(i + 1) * LANES, (i + n) * LANES:(i + n + 1) * LANES] = kern
        for i in range(n, CHUNK) if n else ():
            w1_ref[0, i * LANES:(i + 1) * LANES, (i - n) * LANES:(i - n + 1) * LANES] = zero_tile
        i = CHUNK - 1 - n
        w1_ref[0, i * LANES:(i + 1) * LANES, CW:CW + S] = spread(xr).astype(BF16)
        w1_ref[0, i * LANES:(i + 1) * LANES, CW + S:] = spread(xi).astype(BF16)
    for j in range(CHUNK):
        pr, pi = pw_ref[0, 2 * (j + 1)], pw_ref[0, 2 * (j + 1) + 1]
        wc_ref[0, :S, j * LANES:(j + 1) * LANES] = spread(ctr * pr - cti * pi).T.astype(BF16)
        wc_ref[0, S:, j * LANES:(j + 1) * LANES] = spread(-(ctr * pi + cti * pr)).T.astype(BF16)


def _s5_table_call(pwr, bt, ct):
    n_oct = pwr.shape[0]
    CW = CHUNK * LANES
    S2 = 2 * OCT_STATE
    tile4 = lambda n: pl.BlockSpec((1, n, LANES, LANES), lambda o: (o, 0, 0, 0))
    return pl.pallas_call(
        _s5_table_kernel,
        grid=(n_oct,),
        in_specs=[tile4(pwr.shape[1]), tile4(2), tile4(2)],
        out_specs=[pl.BlockSpec((1, CW, CW + S2), lambda o: (o, 0, 0)),
                   pl.BlockSpec((1, S2, CW), lambda o: (o, 0, 0))],
        out_shape=[jax.ShapeDtypeStruct((n_oct, CW, CW + S2), BF16),
                   jax.ShapeDtypeStruct((n_oct, S2, CW), BF16)],
        compiler_params=pltpu.CompilerParams(
            dimension_semantics=("arbitrary",), vmem_limit_bytes=VMEM_LIMIT_BYTES),
        name="s5_tables",
    )(pwr, bt, ct)


KV_BLOCKS_PER_STEP = 2

def _moba_kernel(q_ref, k_ref, vt_ref, km_ref, o_ref,
                 qt_scr, s0_scr, s1_scr, smax0_scr, smax1_scr, m_scr, acc_scr, *, nblocks):
    t = pl.program_id(2)
    BK = MOBA_BLOCK
    NB = KV_BLOCKS_PER_STEP
    QT = q_ref.shape[0]

    q2t = q_ref[...].astype(F32).T.astype(BF16)
    km = km_ref[0]
    km_hi = km.astype(BF16)
    km_lo = (km - km_hi.astype(F32)).astype(BF16)
    blk = lax.broadcasted_iota(jnp.int32, (nblocks, QT), 0)
    blk_f = blk.astype(F32)
    own = t * (QT // BK) + lax.broadcasted_iota(jnp.int32, (nblocks, QT), 1) // BK
    pad = jnp.zeros((LANES - nblocks, QT), BF16)
    no_feat = jnp.zeros((HEAD_DIM, QT), BF16)
    for a in range(2):
        head = q2t[a * HEAD_DIM:(a + 1) * HEAD_DIM]
        qat = jnp.concatenate([head, no_feat] if a == 0 else [no_feat, head], axis=0)
        gate = _dot(km_hi, qat) + _dot(km_lo, qat)
        gate = jnp.where(blk < own, gate, NEG_INF)
        sel = jnp.zeros(gate.shape, jnp.bool_)
        for _ in range(MOBA_TOPK):
            top = jnp.max(gate, axis=0, keepdims=True)
            idx = jnp.min(jnp.where(gate == top, blk_f, float(nblocks)), axis=0, keepdims=True)
            pick = blk_f == idx
            sel = sel | pick
            gate = jnp.where(pick, -jnp.inf, gate)
        bias = jnp.where((sel & (blk < own)) | (blk == own), 0.0, NEG_INF).astype(BF16)
        qt_scr[a] = jnp.concatenate([qat, bias, pad], axis=0)
        m_scr[a] = jnp.full((1, QT), NEG_INF, F32)
        acc_scr[a] = jnp.zeros((VT_ROWS, QT), F32)

    all_q = slice(0, QT)
    late_q = slice(NB * BK, QT)

    def scores(c, s_scr, smax_scr, qs=all_q):
        kb = k_ref[pl.ds(pl.multiple_of(c * (NB * BK), NB * BK), NB * BK), :]
        for a in range(2):
            s = _dot(kb, qt_scr[a, :, qs])
            s_scr[a, :, qs] = s
            smax_scr[a, :, qs] = jnp.max(s, axis=0, keepdims=True)

    def consume(c, s_scr, smax_scr, qs=all_q):
        for a in range(2):
            vta = jnp.concatenate(
                [vt_ref[c * NB + j, a * VT_ROWS:(a + 1) * VT_ROWS, :] for j in range(NB)], axis=1)
            m_old = m_scr[a, :, qs]
            m_new = jnp.maximum(m_old, smax_scr[a, :, qs])
            p = jnp.exp2(s_scr[a, :, qs] - m_new).astype(BF16)
            acc_scr[a, :, qs] = jnp.exp2(m_old - m_new) * acc_scr[a, :, qs] + _dot(vta, p)
            m_scr[a, :, qs] = m_new

    tri = (lax.broadcasted_iota(jnp.int32, (BK, BK), 0) <= lax.broadcasted_iota(jnp.int32, (BK, BK), 1))

    def causal_patch(e, s_scr, smax_scr, qs=all_q):
        for a in range(2):
            for j in range(NB):
                rows = slice(j * BK, (j + 1) * BK)
                cols = slice((e * NB + j) * BK, (e * NB + j + 1) * BK)
                s_scr[a, rows, cols] = jnp.where(tri, s_scr[a, rows, cols], NEG_INF)
            smax_scr[a, :, qs] = jnp.max(s_scr[a, :, qs], axis=0, keepdims=True)

    buf0 = (s0_scr, smax0_scr)
    buf1 = (s1_scr, smax1_scr)
    scores(0, *buf0)

    def body(i, carry):
        scores(2 * i + 1, *buf1)
        consume(2 * i, *buf0)
        scores(2 * i + 2, *buf0)
        consume(2 * i + 1, *buf1)
        return carry

    lax.fori_loop(0, t, body, 0)
    scores(2 * t + 1, *buf1, qs=late_q)
    causal_patch(0, *buf0)
    consume(2 * t, *buf0)
    causal_patch(1, *buf1, qs=late_q)
    consume(2 * t + 1, *buf1, qs=late_q)
    ot = jnp.concatenate([acc_scr[a, :HEAD_DIM] / acc_scr[a, HEAD_DIM:HEAD_DIM + 1] for a in range(2)], axis=0)
    o_ref[...] = ot.T.astype(BF16)


def _moba(q, k, vt, kmean, *, batch, seq):
    T, attn_w = q.shape
    nblocks = seq // MOBA_BLOCK
    n_pairs = attn_w // LANES
    BK = MOBA_BLOCK
    QT = 2 * KV_BLOCKS_PER_STEP * BK
    tiles = seq // QT
    return pl.pallas_call(
        functools.partial(_moba_kernel, nblocks=nblocks),
        grid=(batch, n_pairs, tiles),
        in_specs=[
            pl.BlockSpec((QT, LANES), lambda b, h, i: (b * tiles + i, h)),
            pl.BlockSpec((seq, 2 * LANES), lambda b, h, i: (b, h)),
            pl.BlockSpec((nblocks, 2 * VT_ROWS, BK), lambda b, h, i: (b, h, 0)),
            pl.BlockSpec((1, nblocks, LANES), lambda b, h, i: (b, 0, h)),
        ],
        out_specs=pl.BlockSpec((QT, LANES), lambda b, h, i: (b * tiles + i, h)),
        out_shape=jax.ShapeDtypeStruct((T, attn_w), BF16),
        scratch_shapes=[
            pltpu.VMEM((2, 2 * LANES, QT), BF16),
            pltpu.VMEM((2, KV_BLOCKS_PER_STEP * BK, QT), F32),
            pltpu.VMEM((2, KV_BLOCKS_PER_STEP * BK, QT), F32),
            pltpu.VMEM((2, 1, QT), F32),
            pltpu.VMEM((2, 1, QT), F32),
            pltpu.VMEM((2, 1, QT), F32),
            pltpu.VMEM((2, VT_ROWS, QT), F32),
        ],
        compiler_params=pltpu.CompilerParams(
            dimension_semantics=("arbitrary", "arbitrary", "arbitrary"), vmem_limit_bytes=VMEM_LIMIT_BYTES),
        name="moba_attn",
    )(q, k, vt, kmean)


def _out_ffn_kernel(x1_ref, y_ref, a_ref, gm_ref, wgate_ref, gluw_ref, glub_ref, wbs_ref, wba_ref, wout_ref,
                    g3_ref, wg_ref, wu_ref, wd_ref, gf_ref, o_ref):
    x1 = x1_ref[...]
    D = x1.shape[1]
    h = _rmsnorm(x1, gm_ref[...]).astype(BF16)
    gates = _dot(h, wgate_ref[...])
    ys = jax.nn.gelu(y_ref[:x1.shape[0], :])
    ys = ys * jax.nn.sigmoid(_dot(ys.astype(BF16), gluw_ref[...]) + glub_ref[...])
    branch_a = _dot(ys.astype(BF16), wbs_ref[...])
    branch_b = _dot(a_ref[...], wba_ref[...])
    merged = jax.nn.sigmoid(gates[:, :D]) * branch_a + jax.nn.sigmoid(gates[:, D:]) * branch_b
    x2 = x1 + _dot(merged.astype(BF16), wout_ref[...])

    h3 = _rmsnorm(x2, g3_ref[...]).astype(BF16)
    act = (jax.nn.silu(_dot(h3, wg_ref[...])) * _dot(h3, wu_ref[...])).astype(BF16)
    x3 = x2 + 0.5 * _dot(act, wd_ref[...])
    o_ref[...] = _rmsnorm(x3, gf_ref[...])


def _out_ffn(x1, y, attn, gm, wgate, gluw, glub, wbs, wba, wout, g3, wg, wu, wd, gf, *, tm):
    T, D = x1.shape
    F = wg.shape[1]
    ssm_w = y.shape[1]
    attn_w = attn.shape[1]
    row = lambda i: (i, 0)
    return pl.pallas_call(
        _out_ffn_kernel,
        grid=(T // tm,),
        in_specs=[
            pl.BlockSpec((tm, D), row), pl.BlockSpec((tm + SEG_PAD, ssm_w), row), pl.BlockSpec((tm, attn_w), row),
            _const_spec((1, D)), _const_spec((D, 2 * D)), _const_spec((ssm_w, ssm_w)), _const_spec((1, ssm_w)),
            _const_spec((ssm_w, D)), _const_spec((attn_w, D)), _const_spec((D, D)),
            _const_spec((1, D)), _const_spec((D, F)), _const_spec((D, F)), _const_spec((F, D)),
            _const_spec((1, D)),
        ],
        out_specs=pl.BlockSpec((tm, D), row),
        out_shape=jax.ShapeDtypeStruct((T, D), F32),
        compiler_params=pltpu.CompilerParams(
            dimension_semantics=("arbitrary",), vmem_limit_bytes=VMEM_LIMIT_BYTES),
        name="out_ffn2",
    )(x1, y, attn, gm, wgate, gluw, glub, wbs, wba, wout, g3, wg, wu, wd, gf)


def _rope_tables(seq):
    pos = jnp.arange(seq, dtype=F32)
    inv_freq = ROPE_THETA ** (-jnp.arange(0, HEAD_DIM, 2, dtype=F32) / HEAD_DIM)
    ang = pos[:, None] * inv_freq[None, :]
    cos = jnp.cos(ang)
    sin = jnp.sin(ang)
    cos_t = jnp.tile(cos, (1, 2 * LANES // HEAD_DIM))
    sin_t = jnp.tile(jnp.concatenate([-sin, sin], axis=1), (1, LANES // HEAD_DIM))
    return cos_t, sin_t


TM_FFN = 512
TM_S5 = 4096


def kernel(x, ffn1_norm, ffn1_w_gate, ffn1_w_up, ffn1_w_down, mix_norm, w_in, ssm_a_re, ssm_a_im, ssm_b_re, ssm_b_im, ssm_c_re, ssm_c_im, ssm_d, ssm_log_dt, glu_w, glu_b, w_branch_ssm, w_branch_attn, w_out, ffn2_norm, ffn2_w_gate, ffn2_w_up, ffn2_w_down, final_norm):
    B, L, D = x.shape
    ssm_w = glu_w.shape[1]
    attn_w = w_branch_attn.shape[1]
    T = B * L
    assert ffn1_norm.shape[0] == 1, "single-layer trunk only"
    assert attn_w == N_HEADS * HEAD_DIM and L % TM_S5 == 0 and T % TM_FFN == 0
    assert TM_FFN % MOBA_BLOCK == 0 and L % TM_FFN == 0
    assert TM_S5 == SUBLANES * TM_FFN, "an S5 segment is one FFN tile (u and y carry SEG_PAD rows after each)"
    bf = lambda w: w[0].astype(BF16)
    row = lambda v: v[0][None]
    cos_t, sin_t = _rope_tables(L)
    nseg = TM_S5 // (SUBLANES * CHUNK)
    wi = w_in[0]
    off_v = ssm_w + 2 * attn_w
    off_g = ssm_w + 3 * attn_w
    gate_w = wi.shape[1] - off_g
    assert off_g % gate_w == 0
    later = [(wi, (gate_w, off_g // gate_w)), (glu_w[0], None), (w_branch_ssm[0], None), (w_branch_attn[0], None),
             (w_out[0], None), (ffn2_w_gate[0], None), (ffn2_w_up[0], None), (ffn2_w_down[0], None)]
    x1, u, q, k, vt, kmean8, *later_bf16 = _ffn_inproj(
        x.reshape(T, D), row(ffn1_norm), bf(ffn1_w_gate), bf(ffn1_w_up), bf(ffn1_w_down), row(mix_norm),
        wi[:, :off_v].astype(BF16), wi[:, off_v:off_g].T.astype(BF16), cos_t, sin_t, later, seq=L, tm=TM_FFN)
    wgate, gluw, wbs, wba, wout, wg2, wu2, wd2 = later_bf16
    w1, wc, a8, pw = _s5_tables(ssm_a_re[0], ssm_a_im[0], ssm_b_re[0], ssm_b_im[0],
                                ssm_c_re[0], ssm_c_im[0], ssm_log_dt[0], nseg)
    y = _s5(u, w1, wc, a8, pw, row(ssm_d), batch=B, seq=L, tm=TM_S5)
    kmean = kmean8[::SUBLANES].reshape(B, L // MOBA_BLOCK, attn_w)
    attn = _moba(q, k, vt, kmean, batch=B, seq=L)
    out = _out_ffn(
        x1, y, attn, row(mix_norm), wgate, gluw, row(glu_b), wbs, wba, wout, row(ffn2_norm), wg2, wu2, wd2,
        final_norm[None], tm=TM_FFN)
    return out.reshape(B, L, D)
```

```python
import functools
import math

import jax
import jax.numpy as jnp
from jax import lax
from jax.experimental import pallas as pl
from jax.experimental.pallas import tpu as pltpu

F32 = jnp.float32
BF16 = jnp.bfloat16

N_HEADS = 8
HEAD_DIM = 64
MOBA_BLOCK = 256
MOBA_TOPK = 3
SSM_GROUP = 16
SSM_STATE = 64
ROPE_THETA = 10000.0
RMS_EPS = 1e-6
NEG_INF = -1e30

LANES = 128
SUBLANES = 8
VMEM_LIMIT_BYTES = 56 * 1024 * 1024

CHUNK = 8
OCTET = LANES // SSM_GROUP
OCT_STATE = OCTET * SSM_STATE

VT_ROWS = HEAD_DIM + 16

SEG_PAD = 8


def _rmsnorm(x, gain):
    inv = lax.rsqrt(jnp.mean(x * x, axis=-1, keepdims=True) + RMS_EPS)
    return (x * inv) * gain


def _dot(a, b):
    return jnp.dot(a, b, preferred_element_type=F32)


def _const_spec(shape):
    nd = len(shape)
    return pl.BlockSpec(shape, lambda *_: (0,) * nd, pipeline_mode=pl.Buffered(1))


def _rotary_tile(x, cos, sin_signed, first_half):
    swapped = jnp.where(first_half, pltpu.roll(x, LANES - HEAD_DIM // 2, 1), pltpu.roll(x, HEAD_DIM // 2, 1))
    return x * cos + swapped * sin_signed


def _ffn_inproj_kernel(x_ref, g1_ref, wg_ref, wu_ref, wd_ref, g2_ref, win_ref, cos_ref, sin_ref, *rest,
                       attn_w, ssm_w, nblocks, cast_slabs, warm):
    n_cast = len(cast_slabs)
    cast_in = rest[:n_cast]
    x1_ref, u_ref, q_ref, k_ref, vt_ref, kmean_ref = rest[n_cast:n_cast + 6]
    cast_out = rest[n_cast + 6:2 * n_cast + 6]
    wg_b, wu_b, wd_b, win_b = rest[2 * n_cast + 6:]
    i = pl.program_id(0)

    @pl.when(i < warm)
    def _():
        for src, dst in ((wg_ref, wg_b), (wu_ref, wu_b), (wd_ref, wd_b), (win_ref, win_b)):
            n = src.shape[0]
            dst[pl.ds(pl.multiple_of(i * n, n), n), :] = src[...].astype(BF16)

    @pl.when(i >= warm)
    def _():
        step = i - warm
        for w_ref, wb_ref, n in zip(cast_in, cast_out, cast_slabs):
            @pl.when(step < n)
            def _():
                wb_ref[...] = w_ref[...].astype(BF16)

        x = x_ref[...]
        h = _rmsnorm(x, g1_ref[...]).astype(BF16)
        act = (jax.nn.silu(_dot(h, wg_b[...])) * _dot(h, wu_b[...])).astype(BF16)
        x1 = x + 0.5 * _dot(act, wd_b[...])
        x1_ref[...] = x1

        h2 = _rmsnorm(x1, g2_ref[...]).astype(BF16)
        proj = _dot(h2, win_b[...])
        tm = x.shape[0]
        u_ref[:tm, :] = proj[:, :ssm_w]
        u_ref[tm:, :] = jnp.zeros((SEG_PAD, ssm_w), F32)

        cos = cos_ref[...]
        sin_signed = sin_ref[...]
        lane = lax.broadcasted_iota(jnp.int32, cos.shape, 1)
        first_half = (lane % HEAD_DIM) < (HEAD_DIM // 2)
        scale = HEAD_DIM ** -0.5 * math.log2(math.e)
        nblk = tm // MOBA_BLOCK
        key_blk = (step * nblk + lax.broadcasted_iota(jnp.int32, cos.shape, 0) // MOBA_BLOCK) % nblocks
        blk_onehot = (lane == key_blk).astype(BF16)
        for t in range(attn_w // LANES):
            qs = proj[:, ssm_w + t * LANES: ssm_w + (t + 1) * LANES]
            ks = proj[:, ssm_w + attn_w + t * LANES: ssm_w + attn_w + (t + 1) * LANES]
            q_ref[:, t * LANES:(t + 1) * LANES] = (
                _rotary_tile(qs, cos, sin_signed, first_half) * scale).astype(BF16)
            kr = _rotary_tile(ks, cos, sin_signed, first_half)
            k_ref[:, 2 * t * LANES:(2 * t + 1) * LANES] = kr.astype(BF16)
            k_ref[:, (2 * t + 1) * LANES:(2 * t + 2) * LANES] = blk_onehot
            for j in range(nblk):
                mean = jnp.mean(kr[j * MOBA_BLOCK:(j + 1) * MOBA_BLOCK], axis=0, keepdims=True)
                kmean_ref[j * SUBLANES:(j + 1) * SUBLANES, t * LANES:(t + 1) * LANES] = jnp.broadcast_to(
                    mean, (SUBLANES, LANES))

        vt = proj[:, ssm_w + 2 * attn_w:].T.astype(BF16)
        ones = jnp.ones((VT_ROWS - HEAD_DIM, MOBA_BLOCK), BF16)
        for j in range(nblk):
            for hd in range(attn_w // HEAD_DIM):
                vt_ref[j, hd * VT_ROWS:hd * VT_ROWS + HEAD_DIM, :] = (
                    vt[hd * HEAD_DIM:(hd + 1) * HEAD_DIM, j * MOBA_BLOCK:(j + 1) * MOBA_BLOCK])
                vt_ref[j, hd * VT_ROWS + HEAD_DIM:(hd + 1) * VT_ROWS, :] = ones


BF16_SUBLANES = 16


def _cast_slabs(rows, steps):
    for n in range(steps, 0, -1):
        if rows % n == 0 and (rows // n) % BF16_SUBLANES == 0:
            return n
    raise ValueError(f"cannot slab {rows} rows over {steps} steps")


WARM_STEPS = 16


def _ffn_inproj(x2d, g1, wg, wu, wd, g2, w_in, cos_t, sin_t, later_weights, *, seq, tm, attn_w, ssm_w):
    T, D = x2d.shape
    F = wg.shape[1]
    in_w = ssm_w + 3 * attn_w
    nblk = tm // MOBA_BLOCK
    tiles_per_seq = seq // tm
    steps = T // tm
    W = WARM_STEPS
    assert D % (W * BF16_SUBLANES) == 0 and F % (W * BF16_SUBLANES) == 0
    tok = lambda i: jnp.maximum(i - W, 0)
    row = lambda i: (tok(i), 0)
    slab = lambda i: (jnp.minimum(i, W - 1), 0)
    cast_in, cast_out, cast_shapes, cast_slabs = [], [], [], []
    for w, cols in later_weights:
        n = _cast_slabs(w.shape[0], steps)
        width, cblk = cols if cols is not None else (w.shape[1], 0)
        cast_in.append(pl.BlockSpec((w.shape[0] // n, width),
                                    lambda i, n=n, cblk=cblk: (jnp.minimum(tok(i), n - 1), cblk)))
        cast_out.append(pl.BlockSpec((w.shape[0] // n, width), lambda i, n=n: (jnp.minimum(tok(i), n - 1), 0)))
        cast_shapes.append(jax.ShapeDtypeStruct((w.shape[0], width), BF16))
        cast_slabs.append(n)
    return pl.pallas_call(
        functools.partial(_ffn_inproj_kernel, attn_w=attn_w, ssm_w=ssm_w, nblocks=seq // MOBA_BLOCK,
                          cast_slabs=tuple(cast_slabs), warm=W),
        grid=(W + steps,),
        in_specs=[
            pl.BlockSpec((tm, D), row),
            _const_spec((1, D)),
            pl.BlockSpec((D // W, F), slab), pl.BlockSpec((D // W, F), slab), pl.BlockSpec((F // W, D), slab),
            _const_spec((1, D)), pl.BlockSpec((D // W, in_w), slab),
            pl.BlockSpec((tm, LANES), lambda i: (tok(i) % tiles_per_seq, 0)),
            pl.BlockSpec((tm, LANES), lambda i: (tok(i) % tiles_per_seq, 0)),
        ] + cast_in,
        out_specs=[
            pl.BlockSpec((tm, D), row),
            pl.BlockSpec((tm + SEG_PAD, ssm_w), row),
            pl.BlockSpec((tm, attn_w), row),
            pl.BlockSpec((tm, 2 * attn_w), row),
            pl.BlockSpec((nblk, attn_w // HEAD_DIM * VT_ROWS, MOBA_BLOCK), lambda i: (tok(i), 0, 0)),
            pl.BlockSpec((nblk * SUBLANES, attn_w), row),
        ] + cast_out,
        out_shape=[
            jax.ShapeDtypeStruct((T, D), F32),
            jax.ShapeDtypeStruct((steps * (tm + SEG_PAD), ssm_w), F32),
            jax.ShapeDtypeStruct((T, attn_w), BF16),
            jax.ShapeDtypeStruct((T, 2 * attn_w), BF16),
            jax.ShapeDtypeStruct((T // MOBA_BLOCK, attn_w // HEAD_DIM * VT_ROWS, MOBA_BLOCK), BF16),
            jax.ShapeDtypeStruct((T // MOBA_BLOCK * SUBLANES, attn_w), F32),
        ] + cast_shapes,
        scratch_shapes=[pltpu.VMEM((D, F), BF16), pltpu.VMEM((D, F), BF16), pltpu.VMEM((F, D), BF16),
                        pltpu.VMEM((D, in_w), BF16)],
        compiler_params=pltpu.CompilerParams(
            dimension_semantics=("arbitrary",), vmem_limit_bytes=VMEM_LIMIT_BYTES),
        name="ffn1_inproj",
    )(x2d, g1, wg, wu, wd, g2, w_in, cos_t, sin_t, *[w for w, _ in later_weights])


def _s5_kernel(u_ref, w1_ref, wc_ref, a8_ref, pw_ref, d_ref, y_ref,
               lhs_scr, z_scr, y_scr, e_scr, ein_scr, carry_scr, *, tm, nseg):
    S = OCT_STATE
    CW = CHUNK * LANES
    seg_stride = tm // SUBLANES + SEG_PAD

    @pl.when(pl.program_id(2) == 0)
    def _():
        carry_scr[...] = jnp.zeros_like(carry_scr)

    for j in range(nseg):
        for i in range(CHUNK):
            lhs_scr[j * SUBLANES:(j + 1) * SUBLANES, i * LANES:(i + 1) * LANES] = (
                u_ref[pl.ds(CHUNK * j + i, SUBLANES, stride=seg_stride), :])

    lhs = lhs_scr[...].astype(BF16)
    z_scr[...] = _dot(lhs, w1_ref[0, :, CW:])
    y_scr[...] = _dot(lhs, w1_ref[0, :, :CW])

    a8 = a8_ref[0]
    ar = jnp.broadcast_to(a8[:, :S], (SUBLANES, S))
    ai = jnp.broadcast_to(a8[:, S:], (SUBLANES, S))
    er = jnp.zeros((SUBLANES, S), F32)
    ei = jnp.zeros((SUBLANES, S), F32)
    for j in range(nseg):
        rows = slice(j * SUBLANES, (j + 1) * SUBLANES)
        e_scr[rows, :S] = er
        e_scr[rows, S:] = ei
        zr = z_scr[rows, :S]
        zi = z_scr[rows, S:]
        er, ei = ar * er - ai * ei + zr, ar * ei + ai * er + zi
    ein_scr[:, :S] = er
    ein_scr[:, S:] = ei

    pw = pw_ref[0]
    pnr = pw[nseg:nseg + 1, :S]
    pni = pw[nseg:nseg + 1, S:]
    cr = carry_scr[0:1, :S]
    ci = carry_scr[0:1, S:]
    for s in range(SUBLANES):
        fr = ein_scr[s:s + 1, :S]
        fi = ein_scr[s:s + 1, S:]
        ein_scr[s:s + 1, :S] = cr
        ein_scr[s:s + 1, S:] = ci
        cr, ci = fr + pnr * cr - pni * ci, fi + pnr * ci + pni * cr
    carry_scr[0:1, :S] = cr
    carry_scr[0:1, S:] = ci

    einr = ein_scr[:, :S]
    eini = ein_scr[:, S:]
    for j in range(nseg):
        rows = slice(j * SUBLANES, (j + 1) * SUBLANES)
        pr = pw[j:j + 1, :S]
        pi = pw[j:j + 1, S:]
        e_scr[rows, :S] = e_scr[rows, :S] + (pr * einr - pi * eini)
        e_scr[rows, S:] = e_scr[rows, S:] + (pr * eini + pi * einr)

    y_scr[...] = y_scr[...] + _dot(e_scr[...].astype(BF16), wc_ref[0])

    d = d_ref[...]
    for j in range(nseg):
        rows = slice(j * SUBLANES, (j + 1) * SUBLANES)
        for i in range(CHUNK):
            cols = slice(i * LANES, (i + 1) * LANES)
            y_ref[pl.ds(CHUNK * j + i, SUBLANES, stride=seg_stride), :] = (
                y_scr[rows, cols] + d * lhs_scr[rows, cols])
    for s in range(SUBLANES):
        y_ref[(s + 1) * seg_stride - SEG_PAD:(s + 1) * seg_stride, :] = jnp.zeros((SEG_PAD, LANES), F32)


def _s5(u, w1, wc, a8, pw, d, *, batch, seq, tm):
    ssm_w = u.shape[1]
    tm_rows = tm + SUBLANES * SEG_PAD
    nseg = tm // (SUBLANES * CHUNK)
    nc = tm // CHUNK
    CW = CHUNK * LANES
    S2 = 2 * OCT_STATE
    n_oct = ssm_w // LANES
    tiles = seq // tm
    return pl.pallas_call(
        functools.partial(_s5_kernel, tm=tm, nseg=nseg),
        grid=(n_oct, batch, tiles),
        in_specs=[
            pl.BlockSpec((tm_rows, LANES), lambda o, b, t: (b * tiles + t, o)),
            pl.BlockSpec((1, CW, CW + S2), lambda o, b, t: (o, 0, 0)),
            pl.BlockSpec((1, S2, CW), lambda o, b, t: (o, 0, 0)),
            pl.BlockSpec((1, 1, S2), lambda o, b, t: (o, 0, 0)),
            pl.BlockSpec((1, nseg + 1, S2), lambda o, b, t: (o, 0, 0)),
            pl.BlockSpec((1, LANES), lambda o, b, t: (0, o)),
        ],
        out_specs=pl.BlockSpec((tm_rows, LANES), lambda o, b, t: (b * tiles + t, o)),
        out_shape=jax.ShapeDtypeStruct(u.shape, F32),
        scratch_shapes=[
            pltpu.VMEM((nc, CW), F32),
            pltpu.VMEM((nc, S2), F32),
            pltpu.VMEM((nc, CW), F32),
            pltpu.VMEM((nc, S2), F32),
            pltpu.VMEM((SUBLANES, S2), F32),
            pltpu.VMEM((SUBLANES, S2), F32),
        ],
        compiler_params=pltpu.CompilerParams(
            dimension_semantics=("arbitrary", "arbitrary", "arbitrary"), vmem_limit_bytes=VMEM_LIMIT_BYTES),
        name="s5_scan",
    )(u, w1, wc, a8, pw, d)


def _s5_tables(a_re, a_im, b_re, b_im, c_re, c_im, log_dt, nseg):
    G, P = a_re.shape
    H = b_re.shape[-1]
    n_oct = G // OCTET
    dt = jnp.exp(log_dt)[:, None]
    lam_r = a_re * dt
    lam_i = a_im * dt

    def powers(n):
        n = n.astype(F32)[:, None, None]
        mag = jnp.exp(lam_r * n)
        return mag * jnp.cos(lam_i * n), mag * jnp.sin(lam_i * n)

    pr, pi = powers(jnp.arange(CHUNK + 1))
    den = a_re * a_re + a_im * a_im
    nr = pr[1] - 1.0
    ni = pi[1]
    fr = ((nr * a_re + ni * a_im) / den)[..., None]
    fi = ((ni * a_re - nr * a_im) / den)[..., None]
    bbr = fr * b_re - fi * b_im
    bbi = fr * b_im + fi * b_re

    def octet_rows(x):
        lead = x.shape[:-3]
        x = x.reshape(lead + (n_oct, OCTET * H, P))
        x = jnp.concatenate([x, x], axis=-1)
        return jnp.moveaxis(x, len(lead), 0)

    bt = octet_rows(jnp.stack([bbr, bbi]).transpose(0, 1, 3, 2))
    ct = octet_rows(jnp.stack([c_re, c_im]))
    pwr = jnp.broadcast_to(jnp.stack([pr, pi], axis=1)[:, :, :, None, :], (CHUNK + 1, 2, G, H, P))
    pwr = octet_rows(pwr).reshape(n_oct, 2 * (CHUNK + 1), LANES, LANES)
    w1, w_c = _s5_table_call(pwr, bt, ct)

    a8 = jnp.concatenate([pr[CHUNK].reshape(n_oct, 1, OCT_STATE), pi[CHUNK].reshape(n_oct, 1, OCT_STATE)], axis=2)
    qr, qi = powers(CHUNK * jnp.arange(nseg + 1))
    pw = jnp.concatenate([qr.reshape(nseg + 1, n_oct, OCT_STATE), qi.reshape(nseg + 1, n_oct, OCT_STATE)], axis=2)
    return w1, w_c, a8, pw.transpose(1, 0, 2)


def _s5_table_kernel(pw_ref, bt_ref, ct_ref, w1_ref, wc_ref):
    S = OCT_STATE
    CW = CHUNK * LANES
    row = lax.broadcasted_iota(jnp.int32, (LANES, LANES), 0)
    lane = lax.broadcasted_iota(jnp.int32, (LANES, LANES), 1)
    same_group = (row // SSM_GROUP) == (lane // SSM_GROUP)
    first_copy = lane < SSM_STATE
    wide_row = lax.broadcasted_iota(jnp.int32, (LANES, S), 0)
    wide_lane = lax.broadcasted_iota(jnp.int32, (LANES, S), 1)
    own_states = (wide_row // SSM_GROUP) == (wide_lane // SSM_STATE)

    def spread(x):
        return jnp.where(own_states, jnp.concatenate([x] * (S // LANES), axis=1), 0.0)

    def nt_dot(a, b):
        return lax.dot_general(a, b, (((1,), (1,)), ((), ())), precision=lax.Precision.HIGHEST,
                               preferred_element_type=F32)

    btr, bti = bt_ref[0, 0], bt_ref[0, 1]
    ctr, cti = ct_ref[0, 0], ct_ref[0, 1]
    ctr_once = jnp.where(first_copy, ctr, 0.0)
    cti_once = jnp.where(first_copy, cti, 0.0)
    zero_tile = jnp.zeros((LANES, LANES), BF16)
    for n in range(CHUNK):
        pr, pi = pw_ref[0, 2 * n], pw_ref[0, 2 * n + 1]
        xr = btr * pr - bti * pi
        xi = btr * pi + bti * pr
        kern = jnp.where(same_group, nt_dot(xr, ctr_once) - nt_dot(xi, cti_once), 0.0).astype(BF16)
        for i in range(CHUNK - n):
            w1_ref[0, i * LANES:(i + 1) * LANES, (i + n) * LANES:(i + n + 1) * LANES] = kern
        for i in range(n, CHUNK) if n else ():
            w1_ref[0, i * LANES:(i + 1) * LANES, (i - n) * LANES:(i - n + 1) * LANES] = zero_tile
        i = CHUNK - 1 - n
        w1_ref[0, i * LANES:(i + 1) * LANES, CW:CW + S] = spread(xr).astype(BF16)
        w1_ref[0, i * LANES:(i + 1) * LANES, CW + S:] = spread(xi).astype(BF16)
    for j in range(CHUNK):
        pr, pi = pw_ref[0, 2 * (j + 1)], pw_ref[0, 2 * (j + 1) + 1]
        wc_ref[0, :S, j * LANES:(j + 1) * LANES] = spread(ctr * pr - cti * pi).T.astype(BF16)
        wc_ref[0, S:, j * LANES:(j + 1) * LANES] = spread(-(ctr * pi + cti * pr)).T.astype(BF16)


def _s5_table_call(pwr, bt, ct):
    n_oct = pwr.shape[0]
    CW = CHUNK * LANES
    S2 = 2 * OCT_STATE
    tile4 = lambda n: pl.BlockSpec((1, n, LANES, LANES), lambda o: (o, 0, 0, 0))
    return pl.pallas_call(
        _s5_table_kernel,
        grid=(n_oct,),
        in_specs=[tile4(pwr.shape[1]), tile4(2), tile4(2)],
        out_specs=[pl.BlockSpec((1, CW, CW + S2), lambda o: (o, 0, 0)),
                   pl.BlockSpec((1, S2, CW), lambda o: (o, 0, 0))],
        out_shape=[jax.ShapeDtypeStruct((n_oct, CW, CW + S2), BF16),
                   jax.ShapeDtypeStruct((n_oct, S2, CW), BF16)],
        compiler_params=pltpu.CompilerParams(
            dimension_semantics=("arbitrary",), vmem_limit_bytes=VMEM_LIMIT_BYTES),
        name="s5_tables",
    )(pwr, bt, ct)


KV_BLOCKS_PER_STEP = 2

def _moba_kernel(q_ref, k_ref, vt_ref, km_ref, o_ref,
                 qt_scr, s0_scr, s1_scr, smax0_scr, smax1_scr, m_scr, acc_scr, *, nblocks):
    t = pl.program_id(2)
    BK = MOBA_BLOCK
    NB = KV_BLOCKS_PER_STEP
    QT = q_ref.shape[0]

    q2t = q_ref[...].astype(F32).T.astype(BF16)
    km = km_ref[0]
    km_hi = km.astype(BF16)
    km_lo = (km - km_hi.astype(F32)).astype(BF16)
    blk = lax.broadcasted_iota(jnp.int32, (nblocks, QT), 0)
    blk_f = blk.astype(F32)
    own = t * (QT // BK) + lax.broadcasted_iota(jnp.int32, (nblocks, QT), 1) // BK
    pad = jnp.zeros((LANES - nblocks, QT), BF16)
    no_feat = jnp.zeros((HEAD_DIM, QT), BF16)
    for a in range(2):
        head = q2t[a * HEAD_DIM:(a + 1) * HEAD_DIM]
        qat = jnp.concatenate([head, no_feat] if a == 0 else [no_feat, head], axis=0)
        gate = _dot(km_hi, qat) + _dot(km_lo, qat)
        gate = jnp.where(blk < own, gate, NEG_INF)
        sel = jnp.zeros(gate.shape, jnp.bool_)
        for _ in range(MOBA_TOPK):
            top = jnp.max(gate, axis=0, keepdims=True)
            idx = jnp.min(jnp.where(gate == top, blk_f, float(nblocks)), axis=0, keepdims=True)
            pick = blk_f == idx
            sel = sel | pick
            gate = jnp.where(pick, -jnp.inf, gate)
        bias = jnp.where((sel & (blk < own)) | (blk == own), 0.0, NEG_INF).astype(BF16)
        qt_scr[a] = jnp.concatenate([qat, bias, pad], axis=0)
        m_scr[a] = jnp.full((1, QT), NEG_INF, F32)
        acc_scr[a] = jnp.zeros((VT_ROWS, QT), F32)

    all_q = slice(0, QT)
    late_q = slice(NB * BK, QT)

    def scores(c, s_scr, smax_scr, qs=all_q):
        kb = k_ref[pl.ds(pl.multiple_of(c * (NB * BK), NB * BK), NB * BK), :]
        for a in range(2):
            s = _dot(kb, qt_scr[a, :, qs])
            s_scr[a, :, qs] = s
            smax_scr[a, :, qs] = jnp.max(s, axis=0, keepdims=True)

    def consume(c, s_scr, smax_scr, qs=all_q):
        for a in range(2):
            vta = jnp.concatenate(
                [vt_ref[c * NB + j, a * VT_ROWS:(a + 1) * VT_ROWS, :] for j in range(NB)], axis=1)
            m_old = m_scr[a, :, qs]
            m_new = jnp.maximum(m_old, smax_scr[a, :, qs])
            p = jnp.exp2(s_scr[a, :, qs] - m_new).astype(BF16)
            acc_scr[a, :, qs] = jnp.exp2(m_old - m_new) * acc_scr[a, :, qs] + _dot(vta, p)
            m_scr[a, :, qs] = m_new

    tri = (lax.broadcasted_iota(jnp.int32, (BK, BK), 0) <= lax.broadcasted_iota(jnp.int32, (BK, BK), 1))

    def causal_patch(e, s_scr, smax_scr, qs=all_q):
        for a in range(2):
            for j in range(NB):
                rows = slice(j * BK, (j + 1) * BK)
                cols = slice((e * NB + j) * BK, (e * NB + j + 1) * BK)
                s_scr[a, rows, cols] = jnp.where(tri, s_scr[a, rows, cols], NEG_INF)
            smax_scr[a, :, qs] = jnp.max(s_scr[a, :, qs], axis=0, keepdims=True)

    buf0 = (s0_scr, smax0_scr)
    buf1 = (s1_scr, smax1_scr)
    scores(0, *buf0)

    def body(i, carry):
        scores(2 * i + 1, *buf1)
        consume(2 * i, *buf0)
        scores(2 * i + 2, *buf0)
        consume(2 * i + 1, *buf1)
        return carry

    lax.fori_loop(0, t, body, 0)
    scores(2 * t + 1, *buf1, qs=late_q)
    causal_patch(0, *buf0)
    consume(2 * t, *buf0)
    causal_patch(1, *buf1, qs=late_q)
    consume(2 * t + 1, *buf1, qs=late_q)
    ot = jnp.concatenate([acc_scr[a, :HEAD_DIM] / acc_scr[a, HEAD_DIM:HEAD_DIM + 1] for a in range(2)], axis=0)
    o_ref[...] = ot.T.astype(BF16)


def _moba(q, k, vt, kmean, *, batch, seq):
    T, attn_w = q.shape
    nblocks = seq // MOBA_BLOCK
    n_pairs = attn_w // LANES
    BK = MOBA_BLOCK
    QT = 2 * KV_BLOCKS_PER_STEP * BK
    tiles = seq // QT
    return pl.pallas_call(
        functools.partial(_moba_kernel, nblocks=nblocks),
        grid=(batch, n_pairs, tiles),
        in_specs=[
            pl.BlockSpec((QT, LANES), lambda b, h, i: (b * tiles + i, h)),
            pl.BlockSpec((seq, 2 * LANES), lambda b, h, i: (b, h)),
            pl.BlockSpec((nblocks, 2 * VT_ROWS, BK), lambda b, h, i: (b, h, 0)),
            pl.BlockSpec((1, nblocks, LANES), lambda b, h, i: (b, 0, h)),
        ],
        out_specs=pl.BlockSpec((QT, LANES), lambda b, h, i: (b * tiles + i, h)),
        out_shape=jax.ShapeDtypeStruct((T, attn_w), BF16),
        scratch_shapes=[
            pltpu.VMEM((2, 2 * LANES, QT), BF16),
            pltpu.VMEM((2, KV_BLOCKS_PER_STEP * BK, QT), F32),
            pltpu.VMEM((2, KV_BLOCKS_PER_STEP * BK, QT), F32),
            pltpu.VMEM((2, 1, QT), F32),
            pltpu.VMEM((2, 1, QT), F32),
            pltpu.VMEM((2, 1, QT), F32),
            pltpu.VMEM((2, VT_ROWS, QT), F32),
        ],
        compiler_params=pltpu.CompilerParams(
            dimension_semantics=("arbitrary", "arbitrary", "arbitrary"), vmem_limit_bytes=VMEM_LIMIT_BYTES),
        name="moba_attn",
    )(q, k, vt, kmean)


def _out_ffn_kernel(x1_ref, y_ref, a_ref, gm_ref, wgate_ref, gluw_ref, glub_ref, wbs_ref, wba_ref, wout_ref,
                    g3_ref, wg_ref, wu_ref, wd_ref, gf_ref, o_ref):
    x1 = x1_ref[...]
    D = x1.shape[1]
    h = _rmsnorm(x1, gm_ref[...]).astype(BF16)
    gates = _dot(h, wgate_ref[...])
    ys = jax.nn.gelu(y_ref[:x1.shape[0], :])
    ys = ys * jax.nn.sigmoid(_dot(ys.astype(BF16), gluw_ref[...]) + glub_ref[...])
    branch_a = _dot(ys.astype(BF16), wbs_ref[...])
    branch_b = _dot(a_ref[...], wba_ref[...])
    merged = jax.nn.sigmoid(gates[:, :D]) * branch_a + jax.nn.sigmoid(gates[:, D:]) * branch_b
    x2 = x1 + _dot(merged.astype(BF16), wout_ref[...])

    h3 = _rmsnorm(x2, g3_ref[...]).astype(BF16)
    act = (jax.nn.silu(_dot(h3, wg_ref[...])) * _dot(h3, wu_ref[...])).astype(BF16)
    x3 = x2 + 0.5 * _dot(act, wd_ref[...])
    o_ref[...] = _rmsnorm(x3, gf_ref[...])


def _out_ffn(x1, y, attn, gm, wgate, gluw, glub, wbs, wba, wout, g3, wg, wu, wd, gf, *, tm):
    T, D = x1.shape
    F = wg.shape[1]
    ssm_w = y.shape[1]
    attn_w = attn.shape[1]
    row = lambda i: (i, 0)
    return pl.pallas_call(
        _out_ffn_kernel,
        grid=(T // tm,),
        in_specs=[
            pl.BlockSpec((tm, D), row), pl.BlockSpec((tm + SEG_PAD, ssm_w), row), pl.BlockSpec((tm, attn_w), row),
            _const_spec((1, D)), _const_spec((D, 2 * D)), _const_spec((ssm_w, ssm_w)), _const_spec((1, ssm_w)),
            _const_spec((ssm_w, D)), _const_spec((attn_w, D)), _const_spec((D, D)),
            _const_spec((1, D)), _const_spec((D, F)), _const_spec((D, F)), _const_spec((F, D)),
            _const_spec((1, D)),
        ],
        out_specs=pl.BlockSpec((tm, D), row),
        out_shape=jax.ShapeDtypeStruct((T, D), F32),
        compiler_params=pltpu.CompilerParams(
            dimension_semantics=("arbitrary",), vmem_limit_bytes=VMEM_LIMIT_BYTES),
        name="out_ffn2",
    )(x1, y, attn, gm, wgate, gluw, glub, wbs, wba, wout, g3, wg, wu, wd, gf)


def _rope_tables(seq):
    pos = jnp.arange(seq, dtype=F32)
    inv_freq = ROPE_THETA ** (-jnp.arange(0, HEAD_DIM, 2, dtype=F32) / HEAD_DIM)
    ang = pos[:, None] * inv_freq[None, :]
    cos = jnp.cos(ang)
    sin = jnp.sin(ang)
    cos_t = jnp.tile(cos, (1, 2 * LANES // HEAD_DIM))
    sin_t = jnp.tile(jnp.concatenate([-sin, sin], axis=1), (1, LANES // HEAD_DIM))
    return cos_t, sin_t


TM_FFN = 512
TM_S5 = 4096


def kernel(x, ffn1_norm, ffn1_w_gate, ffn1_w_up, ffn1_w_down, mix_norm, w_in, ssm_a_re, ssm_a_im, ssm_b_re, ssm_b_im, ssm_c_re, ssm_c_im, ssm_d, ssm_log_dt, glu_w, glu_b, w_branch_ssm, w_branch_attn, w_out, ffn2_norm, ffn2_w_gate, ffn2_w_up, ffn2_w_down, final_norm):
    B, L, D = x.shape
    ssm_w = glu_w.shape[1]
    attn_w = w_branch_attn.shape[1]
    T = B * L
    assert ffn1_norm.shape[0] == 1, "single-layer trunk only"
    assert attn_w == N_HEADS * HEAD_DIM and L % TM_S5 == 0 and T % TM_FFN == 0
    assert TM_FFN % MOBA_BLOCK == 0 and L % TM_FFN == 0
    assert TM_S5 == SUBLANES * TM_FFN, "an S5 segment is one FFN tile (u and y carry SEG_PAD rows after each)"
    bf = lambda w: w[0].astype(BF16)
    row = lambda v: v[0][None]
    cos_t, sin_t = _rope_tables(L)
    nseg = TM_S5 // (SUBLANES * CHUNK)
    wi = w_in[0]
    off_v = ssm_w + 2 * attn_w
    off_g = ssm_w + 3 * attn_w
    gate_w = wi.shape[1] - off_g
    assert off_g % gate_w == 0
    later = [(wi, (gate_w, off_g // gate_w)), (glu_w[0], None), (w_branch_ssm[0], None), (w_branch_attn[0], None),
             (w_out[0], None), (ffn2_w_gate[0], None), (ffn2_w_up[0], None), (ffn2_w_down[0], None)]
    x1, u, q, k, vt, kmean8, *later_bf16 = _ffn_inproj(
        x.reshape(T, D), row(ffn1_norm), ffn1_w_gate[0], ffn1_w_up[0], ffn1_w_down[0], row(mix_norm), wi,
        cos_t, sin_t, later, seq=L, tm=TM_FFN, attn_w=attn_w, ssm_w=ssm_w)
    wgate, gluw, wbs, wba, wout, wg2, wu2, wd2 = later_bf16
    w1, wc, a8, pw = _s5_tables(ssm_a_re[0], ssm_a_im[0], ssm_b_re[0], ssm_b_im[0],
                                ssm_c_re[0], ssm_c_im[0], ssm_log_dt[0], nseg)
    y = _s5(u, w1, wc, a8, pw, row(ssm_d), batch=B, seq=L, tm=TM_S5)
    kmean = kmean8[::SUBLANES].reshape(B, L // MOBA_BLOCK, attn_w)
    attn = _moba(q, k, vt, kmean, batch=B, seq=L)
    out = _out_ffn(
        x1, y, attn, row(mix_norm), wgate, gluw, row(glu_b), wbs, wba, wout, row(ffn2_norm), wg2, wu2, wd2,
        final_norm[None], tm=TM_FFN)
    return out.reshape(B, L, D)
```

```python
import functools
import math

import jax
import jax.numpy as jnp
from jax import lax
from jax.experimental import pallas as pl
from jax.experimental.pallas import tpu as pltpu

F32 = jnp.float32
BF16 = jnp.bfloat16

N_HEADS = 8
HEAD_DIM = 64
MOBA_BLOCK = 256
MOBA_TOPK = 3
SSM_GROUP = 16
SSM_STATE = 64
ROPE_THETA = 10000.0
RMS_EPS = 1e-6
NEG_INF = -1e30

LANES = 128
SUBLANES = 8
VMEM_LIMIT_BYTES = 56 * 1024 * 1024

CHUNK = 8
OCTET = LANES // SSM_GROUP
OCT_STATE = OCTET * SSM_STATE

VT_ROWS = HEAD_DIM + 16

SEG_PAD = 8


def _rmsnorm(x, gain):
    inv = lax.rsqrt(jnp.mean(x * x, axis=-1, keepdims=True) + RMS_EPS)
    return (x * inv) * gain


def _dot(a, b):
    return jnp.dot(a, b, preferred_element_type=F32)


def _const_spec(shape):
    nd = len(shape)
    return pl.BlockSpec(shape, lambda *_: (0,) * nd, pipeline_mode=pl.Buffered(1))


def _rotary_tile(x, cos, sin_signed, first_half):
    swapped = jnp.where(first_half, pltpu.roll(x, LANES - HEAD_DIM // 2, 1), pltpu.roll(x, HEAD_DIM // 2, 1))
    return x * cos + swapped * sin_signed


def _ffn_inproj_kernel(x_ref, g1_ref, wg_ref, wu_ref, wd_ref, g2_ref, win_ref, cos_ref, sin_ref, *rest,
                       attn_w, ssm_w, nblocks, cast_slabs, warm):
    n_cast = len(cast_slabs)
    cast_in = rest[:n_cast]
    x1_ref, u_ref, q_ref, k_ref, vt_ref, kmean_ref = rest[n_cast:n_cast + 6]
    cast_out = rest[n_cast + 6:2 * n_cast + 6]
    wg_b, wu_b, wd_b, win_b = rest[2 * n_cast + 6:]
    i = pl.program_id(0)

    @pl.when(i < warm)
    def _():
        for src, dst in ((wg_ref, wg_b), (wu_ref, wu_b), (wd_ref, wd_b), (win_ref, win_b)):
            n = src.shape[0]
            dst[pl.ds(pl.multiple_of(i * n, n), n), :] = src[...].astype(BF16)

    @pl.when(i >= warm)
    def _():
        step = i - warm
        for w_ref, wb_ref, n in zip(cast_in, cast_out, cast_slabs):
            @pl.when(step < n)
            def _():
                wb_ref[...] = w_ref[...].astype(BF16)

        x = x_ref[...]
        h = _rmsnorm(x, g1_ref[...]).astype(BF16)
        act = (jax.nn.silu(_dot(h, wg_b[...])) * _dot(h, wu_b[...])).astype(BF16)
        x1 = x + 0.5 * _dot(act, wd_b[...])
        x1_ref[...] = x1

        h2 = _rmsnorm(x1, g2_ref[...]).astype(BF16)
        proj = _dot(h2, win_b[...])
        tm = x.shape[0]
        u_ref[:tm, :] = proj[:, :ssm_w]
        u_ref[tm:, :] = jnp.zeros((SEG_PAD, ssm_w), F32)

        cos = cos_ref[...]
        sin_signed = sin_ref[...]
        lane = lax.broadcasted_iota(jnp.int32, cos.shape, 1)
        first_half = (lane % HEAD_DIM) < (HEAD_DIM // 2)
        scale = HEAD_DIM ** -0.5 * math.log2(math.e)
        nblk = tm // MOBA_BLOCK
        key_blk = (step * nblk + lax.broadcasted_iota(jnp.int32, cos.shape, 0) // MOBA_BLOCK) % nblocks
        blk_onehot = (lane == key_blk).astype(BF16)
        for t in range(attn_w // LANES):
            qs = proj[:, ssm_w + t * LANES: ssm_w + (t + 1) * LANES]
            ks = proj[:, ssm_w + attn_w + t * LANES: ssm_w + attn_w + (t + 1) * LANES]
            q_ref[:, t * LANES:(t + 1) * LANES] = (
                _rotary_tile(qs, cos, sin_signed, first_half) * scale).astype(BF16)
            kr = _rotary_tile(ks, cos, sin_signed, first_half)
            k_ref[:, 2 * t * LANES:(2 * t + 1) * LANES] = kr.astype(BF16)
            k_ref[:, (2 * t + 1) * LANES:(2 * t + 2) * LANES] = blk_onehot
            for j in range(nblk):
                mean = jnp.mean(kr[j * MOBA_BLOCK:(j + 1) * MOBA_BLOCK], axis=0, keepdims=True)
                kmean_ref[j * SUBLANES:(j + 1) * SUBLANES, t * LANES:(t + 1) * LANES] = jnp.broadcast_to(
                    mean, (SUBLANES, LANES))

        vt = proj[:, ssm_w + 2 * attn_w:].T.astype(BF16)
        ones = jnp.ones((VT_ROWS - HEAD_DIM, MOBA_BLOCK), BF16)
        for j in range(nblk):
            for hd in range(attn_w // HEAD_DIM):
                vt_ref[j, hd * VT_ROWS:hd * VT_ROWS + HEAD_DIM, :] = (
                    vt[hd * HEAD_DIM:(hd + 1) * HEAD_DIM, j * MOBA_BLOCK:(j + 1) * MOBA_BLOCK])
                vt_ref[j, hd * VT_ROWS + HEAD_DIM:(hd + 1) * VT_ROWS, :] = ones


BF16_SUBLANES = 16


def _cast_slabs(rows, steps):
    for n in range(steps, 0, -1):
        if rows % n == 0 and (rows // n) % BF16_SUBLANES == 0:
            return n
    raise ValueError(f"cannot slab {rows} rows over {steps} steps")


WARM_STEPS = 16


def _ffn_inproj(x2d, g1, wg, wu, wd, g2, w_in, cos_t, sin_t, later_weights, *, seq, tm, attn_w, ssm_w):
    T, D = x2d.shape
    F = wg.shape[1]
    in_w = ssm_w + 3 * attn_w
    nblk = tm // MOBA_BLOCK
    tiles_per_seq = seq // tm
    steps = T // tm
    W = WARM_STEPS
    assert D % (W * BF16_SUBLANES) == 0 and F % (W * BF16_SUBLANES) == 0
    tok = lambda i: jnp.maximum(i - W, 0)
    row = lambda i: (tok(i), 0)
    slab = lambda i: (jnp.minimum(i, W - 1), 0)
    cast_in, cast_out, cast_shapes, cast_slabs = [], [], [], []
    for w, cols in later_weights:
        n = _cast_slabs(w.shape[0], steps)
        width, cblk = cols if cols is not None else (w.shape[1], 0)
        cast_in.append(pl.BlockSpec((w.shape[0] // n, width),
                                    lambda i, n=n, cblk=cblk: (jnp.minimum(tok(i), n - 1), cblk)))
        cast_out.append(pl.BlockSpec((w.shape[0] // n, width), lambda i, n=n: (jnp.minimum(tok(i), n - 1), 0)))
        cast_shapes.append(jax.ShapeDtypeStruct((w.shape[0], width), BF16))
        cast_slabs.append(n)
    return pl.pallas_call(
        functools.partial(_ffn_inproj_kernel, attn_w=attn_w, ssm_w=ssm_w, nblocks=seq // MOBA_BLOCK,
                          cast_slabs=tuple(cast_slabs), warm=W),
        grid=(W + steps,),
        in_specs=[
            pl.BlockSpec((tm, D), row),
            _const_spec((1, D)),
            pl.BlockSpec((D // W, F), slab), pl.BlockSpec((D // W, F), slab), pl.BlockSpec((F // W, D), slab),
            _const_spec((1, D)), pl.BlockSpec((D // W, in_w), slab),
            pl.BlockSpec((tm, LANES), lambda i: (tok(i) % tiles_per_seq, 0)),
            pl.BlockSpec((tm, LANES), lambda i: (tok(i) % tiles_per_seq, 0)),
        ] + cast_in,
        out_specs=[
            pl.BlockSpec((tm, D), row),
            pl.BlockSpec((tm + SEG_PAD, ssm_w), row),
            pl.BlockSpec((tm, attn_w), row),
            pl.BlockSpec((tm, 2 * attn_w), row),
            pl.BlockSpec((nblk, attn_w // HEAD_DIM * VT_ROWS, MOBA_BLOCK), lambda i: (tok(i), 0, 0)),
            pl.BlockSpec((nblk * SUBLANES, attn_w), row),
        ] + cast_out,
        out_shape=[
            jax.ShapeDtypeStruct((T, D), F32),
            jax.ShapeDtypeStruct((steps * (tm + SEG_PAD), ssm_w), F32),
            jax.ShapeDtypeStruct((T, attn_w), BF16),
            jax.ShapeDtypeStruct((T, 2 * attn_w), BF16),
            jax.ShapeDtypeStruct((T // MOBA_BLOCK, attn_w // HEAD_DIM * VT_ROWS, MOBA_BLOCK), BF16),
            jax.ShapeDtypeStruct((T // MOBA_BLOCK * SUBLANES, attn_w), F32),
        ] + cast_shapes,
        scratch_shapes=[pltpu.VMEM((D, F), BF16), pltpu.VMEM((D, F), BF16), pltpu.VMEM((F, D), BF16),
                        pltpu.VMEM((D, in_w), BF16)],
        compiler_params=pltpu.CompilerParams(
            dimension_semantics=("arbitrary",), vmem_limit_bytes=VMEM_LIMIT_BYTES),
        name="ffn1_inproj",
    )(x2d, g1, wg, wu, wd, g2, w_in, cos_t, sin_t, *[w for w, _ in later_weights])


def _s5_kernel(u_ref, w1_ref, wc_ref, a8_ref, pw_ref, d_ref, y_ref,
               lhs_scr, z_scr, y_scr, e_scr, ein_scr, carry_scr, *, tm, nseg):
    S = OCT_STATE
    CW = CHUNK * LANES
    seg_stride = tm // SUBLANES + SEG_PAD

    @pl.when(pl.program_id(2) == 0)
    def _():
        carry_scr[...] = jnp.zeros_like(carry_scr)

    for j in range(nseg):
        for i in range(CHUNK):
            lhs_scr[j * SUBLANES:(j + 1) * SUBLANES, i * LANES:(i + 1) * LANES] = (
                u_ref[pl.ds(CHUNK * j + i, SUBLANES, stride=seg_stride), :])

    lhs = lhs_scr[...].astype(BF16)
    z_scr[...] = _dot(lhs, w1_ref[0, :, CW:])
    y_scr[...] = _dot(lhs, w1_ref[0, :, :CW])

    a8 = a8_ref[0]
    ar = jnp.broadcast_to(a8[:, :S], (SUBLANES, S))
    ai = jnp.broadcast_to(a8[:, S:], (SUBLANES, S))
    er = jnp.zeros((SUBLANES, S), F32)
    ei = jnp.zeros((SUBLANES, S), F32)
    for j in range(nseg):
        rows = slice(j * SUBLANES, (j + 1) * SUBLANES)
        e_scr[rows, :S] = er
        e_scr[rows, S:] = ei
        zr = z_scr[rows, :S]
        zi = z_scr[rows, S:]
        er, ei = ar * er - ai * ei + zr, ar * ei + ai * er + zi
    ein_scr[:, :S] = er
    ein_scr[:, S:] = ei

    pw = pw_ref[0]
    pnr = pw[nseg:nseg + 1, :S]
    pni = pw[nseg:nseg + 1, S:]
    cr = carry_scr[0:1, :S]
    ci = carry_scr[0:1, S:]
    for s in range(SUBLANES):
        fr = ein_scr[s:s + 1, :S]
        fi = ein_scr[s:s + 1, S:]
        ein_scr[s:s + 1, :S] = cr
        ein_scr[s:s + 1, S:] = ci
        cr, ci = fr + pnr * cr - pni * ci, fi + pnr * ci + pni * cr
    carry_scr[0:1, :S] = cr
    carry_scr[0:1, S:] = ci

    einr = ein_scr[:, :S]
    eini = ein_scr[:, S:]
    for j in range(nseg):
        rows = slice(j * SUBLANES, (j + 1) * SUBLANES)
        pr = pw[j:j + 1, :S]
        pi = pw[j:j + 1, S:]
        e_scr[rows, :S] = e_scr[rows, :S] + (pr * einr - pi * eini)
        e_scr[rows, S:] = e_scr[rows, S:] + (pr * eini + pi * einr)

    y_scr[...] = y_scr[...] + _dot(e_scr[...].astype(BF16), wc_ref[0])

    d = d_ref[...]
    for j in range(nseg):
        rows = slice(j * SUBLANES, (j + 1) * SUBLANES)
        for i in range(CHUNK):
            cols = slice(i * LANES, (i + 1) * LANES)
            y_ref[pl.ds(CHUNK * j + i, SUBLANES, stride=seg_stride), :] = (
                y_scr[rows, cols] + d * lhs_scr[rows, cols])
    for s in range(SUBLANES):
        y_ref[(s + 1) * seg_stride - SEG_PAD:(s + 1) * seg_stride, :] = jnp.zeros((SEG_PAD, LANES), F32)


def _s5(u, w1, wc, a8, pw, d, *, batch, seq, tm):
    ssm_w = u.shape[1]
    tm_rows = tm + SUBLANES * SEG_PAD
    nseg = tm // (SUBLANES * CHUNK)
    nc = tm // CHUNK
    CW = CHUNK * LANES
    S2 = 2 * OCT_STATE
    n_oct = ssm_w // LANES
    tiles = seq // tm
    return pl.pallas_call(
        functools.partial(_s5_kernel, tm=tm, nseg=nseg),
        grid=(n_oct, batch, tiles),
        in_specs=[
            pl.BlockSpec((tm_rows, LANES), lambda o, b, t: (b * tiles + t, o)),
            pl.BlockSpec((1, CW, CW + S2), lambda o, b, t: (o, 0, 0)),
            pl.BlockSpec((1, S2, CW), lambda o, b, t: (o, 0, 0)),
            pl.BlockSpec((1, 1, S2), lambda o, b, t: (o, 0, 0)),
            pl.BlockSpec((1, nseg + 1, S2), lambda o, b, t: (o, 0, 0)),
            pl.BlockSpec((1, LANES), lambda o, b, t: (0, o)),
        ],
        out_specs=pl.BlockSpec((tm_rows, LANES), lambda o, b, t: (b * tiles + t, o)),
        out_shape=jax.ShapeDtypeStruct(u.shape, F32),
        scratch_shapes=[
            pltpu.VMEM((nc, CW), F32),
            pltpu.VMEM((nc, S2), F32),
            pltpu.VMEM((nc, CW), F32),
            pltpu.VMEM((nc, S2), F32),
            pltpu.VMEM((SUBLANES, S2), F32),
            pltpu.VMEM((SUBLANES, S2), F32),
        ],
        compiler_params=pltpu.CompilerParams(
            dimension_semantics=("arbitrary", "arbitrary", "arbitrary"), vmem_limit_bytes=VMEM_LIMIT_BYTES),
        name="s5_scan",
    )(u, w1, wc, a8, pw, d)


def _s5_tables(a_re, a_im, b_re, b_im, c_re, c_im, log_dt, nseg):
    G, P = a_re.shape
    H = b_re.shape[-1]
    n_oct = G // OCTET
    dt = jnp.exp(log_dt)[:, None]
    lam_r = a_re * dt
    lam_i = a_im * dt

    def powers(n):
        n = n.astype(F32)[:, None, None]
        mag = jnp.exp(lam_r * n)
        return mag * jnp.cos(lam_i * n), mag * jnp.sin(lam_i * n)

    pr, pi = powers(jnp.arange(CHUNK + 1))
    den = a_re * a_re + a_im * a_im
    nr = pr[1] - 1.0
    ni = pi[1]
    fr = ((nr * a_re + ni * a_im) / den)[..., None]
    fi = ((ni * a_re - nr * a_im) / den)[..., None]
    bbr = fr * b_re - fi * b_im
    bbi = fr * b_im + fi * b_re

    def octet_rows(x):
        lead = x.shape[:-3]
        x = x.reshape(lead + (n_oct, OCTET * H, P))
        x = jnp.concatenate([x, x], axis=-1)
        return jnp.moveaxis(x, len(lead), 0)

    bt = octet_rows(jnp.stack([bbr, bbi]).transpose(0, 1, 3, 2))
    ct = octet_rows(jnp.stack([c_re, c_im]))
    pwr = jnp.broadcast_to(jnp.stack([pr, pi], axis=1)[:, :, :, None, :], (CHUNK + 1, 2, G, H, P))
    pwr = octet_rows(pwr).reshape(n_oct, 2 * (CHUNK + 1), LANES, LANES)
    w1, w_c = _s5_table_call(pwr, bt, ct)

    a8 = jnp.concatenate([pr[CHUNK].reshape(n_oct, 1, OCT_STATE), pi[CHUNK].reshape(n_oct, 1, OCT_STATE)], axis=2)
    qr, qi = powers(CHUNK * jnp.arange(nseg + 1))
    pw = jnp.concatenate([qr.reshape(nseg + 1, n_oct, OCT_STATE), qi.reshape(nseg + 1, n_oct, OCT_STATE)], axis=2)
    return w1, w_c, a8, pw.transpose(1, 0, 2)


def _s5_table_kernel(pw_ref, bt_ref, ct_ref, w1_ref, wc_ref):
    S = OCT_STATE
    CW = CHUNK * LANES
    row = lax.broadcasted_iota(jnp.int32, (LANES, LANES), 0)
    lane = lax.broadcasted_iota(jnp.int32, (LANES, LANES), 1)
    same_group = (row // SSM_GROUP) == (lane // SSM_GROUP)
    first_copy = lane < SSM_STATE
    wide_row = lax.broadcasted_iota(jnp.int32, (LANES, S), 0)
    wide_lane = lax.broadcasted_iota(jnp.int32, (LANES, S), 1)
    own_states = (wide_row // SSM_GROUP) == (wide_lane // SSM_STATE)

    def spread(x):
        return jnp.where(own_states, jnp.concatenate([x] * (S // LANES), axis=1), 0.0)

    def nt_dot(a, b):
        return lax.dot_general(a, b, (((1,), (1,)), ((), ())), precision=lax.Precision.HIGHEST,
                               preferred_element_type=F32)

    btr, bti = bt_ref[0, 0], bt_ref[0, 1]
    ctr, cti = ct_ref[0, 0], ct_ref[0, 1]
    ctr_once = jnp.where(first_copy, ctr, 0.0)
    cti_once = jnp.where(first_copy, cti, 0.0)
    zero_tile = jnp.zeros((LANES, LANES), BF16)
    for n in range(CHUNK):
        pr, pi = pw_ref[0, 2 * n], pw_ref[0, 2 * n + 1]
        xr = btr * pr - bti * pi
        xi = btr * pi + bti * pr
        kern = jnp.where(same_group, nt_dot(xr, ctr_once) - nt_dot(xi, cti_once), 0.0).astype(BF16)
        for i in range(CHUNK - n):
            w1_ref[0, i * LANES:(i + 1) * LANES, (i + n) * LANES:(i + n + 1) * LANES] = kern
        for i in range(n, CHUNK) if n else ():
            w1_ref[0, i * LANES:(i + 1) * LANES, (i - n) * LANES:(i - n + 1) * LANES] = zero_tile
        i = CHUNK - 1 - n
        w1_ref[0, i * LANES:(i + 1) * LANES, CW:CW + S] = spread(xr).astype(BF16)
        w1_ref[0, i * LANES:(i + 1) * LANES, CW + S:] = spread(xi).astype(BF16)
    for j in range(CHUNK):
        pr, pi = pw_ref[0, 2 * (j + 1)], pw_ref[0, 2 * (j + 1) + 1]
        wc_ref[0, :S, j * LANES:(j + 1) * LANES] = spread(ctr * pr - cti * pi).T.astype(BF16)
        wc_ref[0, S:, j * LANES:(j + 1) * LANES] = spread(-(ctr * pi + cti * pr)).T.astype(BF16)


def _s5_table_call(pwr, bt, ct):
    n_oct = pwr.shape[0]
    CW = CHUNK * LANES
    S2 = 2 * OCT_STATE
    tile4 = lambda n: pl.BlockSpec((1, n, LANES, LANES), lambda o: (o, 0, 0, 0))
    return pl.pallas_call(
        _s5_table_kernel,
        grid=(n_oct,),
        in_specs=[tile4(pwr.shape[1]), tile4(2), tile4(2)],
        out_specs=[pl.BlockSpec((1, CW, CW + S2), lambda o: (o, 0, 0)),
                   pl.BlockSpec((1, S2, CW), lambda o: (o, 0, 0))],
        out_shape=[jax.ShapeDtypeStruct((n_oct, CW, CW + S2), BF16),
                   jax.ShapeDtypeStruct((n_oct, S2, CW), BF16)],
        compiler_params=pltpu.CompilerParams(
            dimension_semantics=("arbitrary",), vmem_limit_bytes=VMEM_LIMIT_BYTES),
        name="s5_tables",
    )(pwr, bt, ct)


KV_BLOCKS_PER_STEP = 2


def _moba_kernel(q_ref, k_ref, vt_ref, km_ref, o_ref,
                 qt_scr, s0_scr, s1_scr, smax0_scr, smax1_scr, m_scr, acc_scr, *, nblocks):
    t = pl.program_id(2)
    BK = MOBA_BLOCK
    NB = KV_BLOCKS_PER_STEP
    QT = q_ref.shape[0]

    q2t = q_ref[...].astype(F32).T.astype(BF16)
    km = km_ref[0]
    km_hi = km.astype(BF16)
    km_lo = (km - km_hi.astype(F32)).astype(BF16)
    blk = lax.broadcasted_iota(jnp.int32, (nblocks, QT), 0)
    blk_f = blk.astype(F32)
    own = t * (QT // BK) + lax.broadcasted_iota(jnp.int32, (nblocks, QT), 1) // BK
    pad = jnp.zeros((LANES - nblocks, QT), BF16)
    no_feat = jnp.zeros((HEAD_DIM, QT), BF16)
    for a in range(2):
        head = q2t[a * HEAD_DIM:(a + 1) * HEAD_DIM]
        qat = jnp.concatenate([head, no_feat] if a == 0 else [no_feat, head], axis=0)
        gate = _dot(km_hi, qat) + _dot(km_lo, qat)
        gate = jnp.where(blk < own, gate, NEG_INF)
        sel = jnp.zeros(gate.shape, jnp.bool_)
        for _ in range(MOBA_TOPK):
            top = jnp.max(gate, axis=0, keepdims=True)
            idx = jnp.min(jnp.where(gate == top, blk_f, float(nblocks)), axis=0, keepdims=True)
            pick = blk_f == idx
            sel = sel | pick
            gate = jnp.where(pick, -jnp.inf, gate)
        bias = jnp.where((sel & (blk < own)) | (blk == own), 0.0, NEG_INF).astype(BF16)
        qt_scr[a] = jnp.concatenate([qat, bias, pad], axis=0)
        m_scr[a] = jnp.full((1, QT), NEG_INF, F32)
        acc_scr[a] = jnp.zeros((VT_ROWS, QT), F32)

    all_q = slice(0, QT)
    late_q = slice(NB * BK, QT)

    def scores(c, s_scr, smax_scr, qs=all_q):
        kb = k_ref[pl.ds(pl.multiple_of(c * (NB * BK), NB * BK), NB * BK), :]
        for a in range(2):
            s = _dot(kb, qt_scr[a, :, qs])
            s_scr[a, :, qs] = s
            smax_scr[a, :, qs] = jnp.max(s, axis=0, keepdims=True)

    def consume(c, s_scr, smax_scr, qs=all_q):
        for a in range(2):
            vta = jnp.concatenate(
                [vt_ref[c * NB + j, a * VT_ROWS:(a + 1) * VT_ROWS, :] for j in range(NB)], axis=1)
            m_old = m_scr[a, :, qs]
            m_new = jnp.maximum(m_old, smax_scr[a, :, qs])
            p = jnp.exp2(s_scr[a, :, qs] - m_new).astype(BF16)
            acc_scr[a, :, qs] = jnp.exp2(m_old - m_new) * acc_scr[a, :, qs] + _dot(vta, p)
            m_scr[a, :, qs] = m_new

    tri = (lax.broadcasted_iota(jnp.int32, (BK, BK), 0) <= lax.broadcasted_iota(jnp.int32, (BK, BK), 1))

    def causal_patch(e, s_scr, smax_scr, qs=all_q):
        for a in range(2):
            for j in range(NB):
                rows = slice(j * BK, (j + 1) * BK)
                cols = slice((e * NB + j) * BK, (e * NB + j + 1) * BK)
                s_scr[a, rows, cols] = jnp.where(tri, s_scr[a, rows, cols], NEG_INF)
            smax_scr[a, :, qs] = jnp.max(s_scr[a, :, qs], axis=0, keepdims=True)

    buf0 = (s0_scr, smax0_scr)
    buf1 = (s1_scr, smax1_scr)
    scores(0, *buf0)

    def body(i, carry):
        scores(2 * i + 1, *buf1)
        consume(2 * i, *buf0)
        scores(2 * i + 2, *buf0)
        consume(2 * i + 1, *buf1)
        return carry

    lax.fori_loop(0, t, body, 0)
    scores(2 * t + 1, *buf1, qs=late_q)
    causal_patch(0, *buf0)
    consume(2 * t, *buf0)
    causal_patch(1, *buf1, qs=late_q)
    consume(2 * t + 1, *buf1, qs=late_q)
    ot = jnp.concatenate([acc_scr[a, :HEAD_DIM] / acc_scr[a, HEAD_DIM:HEAD_DIM + 1] for a in range(2)], axis=0)
    o_ref[...] = ot.T.astype(BF16)


def _moba(q, k, vt, kmean, *, batch, seq):
    T, attn_w = q.shape
    nblocks = seq // MOBA_BLOCK
    n_pairs = attn_w // LANES
    BK = MOBA_BLOCK
    QT = 2 * KV_BLOCKS_PER_STEP * BK
    tiles = seq // QT
    return pl.pallas_call(
        functools.partial(_moba_kernel, nblocks=nblocks),
        grid=(batch, n_pairs, tiles),
        in_specs=[
            pl.BlockSpec((QT, LANES), lambda b, h, i: (b * tiles + i, h)),
            pl.BlockSpec((seq, 2 * LANES), lambda b, h, i: (b, h)),
            pl.BlockSpec((nblocks, 2 * VT_ROWS, BK), lambda b, h, i: (b, h, 0)),
            pl.BlockSpec((1, nblocks, LANES), lambda b, h, i: (b, 0, h)),
        ],
        out_specs=pl.BlockSpec((QT, LANES), lambda b, h, i: (b * tiles + i, h)),
        out_shape=jax.ShapeDtypeStruct((T, attn_w), BF16),
        scratch_shapes=[
            pltpu.VMEM((2, 2 * LANES, QT), BF16),
            pltpu.VMEM((2, KV_BLOCKS_PER_STEP * BK, QT), F32),
            pltpu.VMEM((2, KV_BLOCKS_PER_STEP * BK, QT), F32),
            pltpu.VMEM((2, 1, QT), F32),
            pltpu.VMEM((2, 1, QT), F32),
            pltpu.VMEM((2, 1, QT), F32),
            pltpu.VMEM((2, VT_ROWS, QT), F32),
        ],
        compiler_params=pltpu.CompilerParams(
            dimension_semantics=("arbitrary", "arbitrary", "arbitrary"), vmem_limit_bytes=VMEM_LIMIT_BYTES),
        name="moba_attn",
    )(q, k, vt, kmean)


def _out_ffn_kernel(x1_ref, y_ref, a_ref, gm_ref, wgate_ref, gluw_ref, glub_ref, wbs_ref, wba_ref, wout_ref,
                    g3_ref, wg_ref, wu_ref, wd_ref, gf_ref, o_ref):
    x1 = x1_ref[...]
    D = x1.shape[1]
    h = _rmsnorm(x1, gm_ref[...]).astype(BF16)
    gates = _dot(h, wgate_ref[...])
    ys = jax.nn.gelu(y_ref[:x1.shape[0], :])
    ys = ys * jax.nn.sigmoid(_dot(ys.astype(BF16), gluw_ref[...]) + glub_ref[...])
    branch_a = _dot(ys.astype(BF16), wbs_ref[...])
    branch_b = _dot(a_ref[...], wba_ref[...])
    merged = jax.nn.sigmoid(gates[:, :D]) * branch_a + jax.nn.sigmoid(gates[:, D:]) * branch_b
    x2 = x1 + _dot(merged.astype(BF16), wout_ref[...])

    h3 = _rmsnorm(x2, g3_ref[...]).astype(BF16)
    act = (jax.nn.silu(_dot(h3, wg_ref[...])) * _dot(h3, wu_ref[...])).astype(BF16)
    x3 = x2 + 0.5 * _dot(act, wd_ref[...])
    o_ref[...] = _rmsnorm(x3, gf_ref[...])


def _out_ffn(x1, y, attn, gm, wgate, gluw, glub, wbs, wba, wout, g3, wg, wu, wd, gf, *, tm):
    T, D = x1.shape
    F = wg.shape[1]
    ssm_w = y.shape[1]
    attn_w = attn.shape[1]
    row = lambda i: (i, 0)
    return pl.pallas_call(
        _out_ffn_kernel,
        grid=(T // tm,),
        in_specs=[
            pl.BlockSpec((tm, D), row), pl.BlockSpec((tm + SEG_PAD, ssm_w), row), pl.BlockSpec((tm, attn_w), row),
            _const_spec((1, D)), _const_spec((D, 2 * D)), _const_spec((ssm_w, ssm_w)), _const_spec((1, ssm_w)),
            _const_spec((ssm_w, D)), _const_spec((attn_w, D)), _const_spec((D, D)),
            _const_spec((1, D)), _const_spec((D, F)), _const_spec((D, F)), _const_spec((F, D)),
            _const_spec((1, D)),
        ],
        out_specs=pl.BlockSpec((tm, D), row),
        out_shape=jax.ShapeDtypeStruct((T, D), F32),
        compiler_params=pltpu.CompilerParams(
            dimension_semantics=("arbitrary",), vmem_limit_bytes=VMEM_LIMIT_BYTES),
        name="out_ffn2",
    )(x1, y, attn, gm, wgate, gluw, glub, wbs, wba, wout, g3, wg, wu, wd, gf)


def _rope_tables(seq):
    pos = jnp.arange(seq, dtype=F32)
    inv_freq = ROPE_THETA ** (-jnp.arange(0, HEAD_DIM, 2, dtype=F32) / HEAD_DIM)
    ang = pos[:, None] * inv_freq[None, :]
    cos = jnp.cos(ang)
    sin = jnp.sin(ang)
    cos_t = jnp.tile(cos, (1, 2 * LANES // HEAD_DIM))
    sin_t = jnp.tile(jnp.concatenate([-sin, sin], axis=1), (1, LANES // HEAD_DIM))
    return cos_t, sin_t


TM_FFN = 512
TM_S5 = 4096


def kernel(x, ffn1_norm, ffn1_w_gate, ffn1_w_up, ffn1_w_down, mix_norm, w_in, ssm_a_re, ssm_a_im, ssm_b_re, ssm_b_im, ssm_c_re, ssm_c_im, ssm_d, ssm_log_dt, glu_w, glu_b, w_branch_ssm, w_branch_attn, w_out, ffn2_norm, ffn2_w_gate, ffn2_w_up, ffn2_w_down, final_norm):
    B, L, D = x.shape
    ssm_w = glu_w.shape[1]
    attn_w = w_branch_attn.shape[1]
    T = B * L
    assert ffn1_norm.shape[0] == 1, "single-layer trunk only"
    assert attn_w == N_HEADS * HEAD_DIM and L % TM_S5 == 0 and T % TM_FFN == 0
    assert TM_FFN % MOBA_BLOCK == 0 and L % TM_FFN == 0
    assert TM_S5 == SUBLANES * TM_FFN, "an S5 segment is one FFN tile (u and y carry SEG_PAD rows after each)"
    row = lambda v: v[0][None]
    cos_t, sin_t = _rope_tables(L)
    nseg = TM_S5 // (SUBLANES * CHUNK)
    wi = w_in[0]
    off_g = ssm_w + 3 * attn_w
    gate_w = wi.shape[1] - off_g
    assert off_g % gate_w == 0
    later = [(wi, (gate_w, off_g // gate_w)), (glu_w[0], None), (w_branch_ssm[0], None), (w_branch_attn[0], None),
             (w_out[0], None), (ffn2_w_gate[0], None), (ffn2_w_up[0], None), (ffn2_w_down[0], None)]
    x1, u, q, k, vt, kmean8, *later_bf16 = _ffn_inproj(
        x.reshape(T, D), row(ffn1_norm), ffn1_w_gate[0], ffn1_w_up[0], ffn1_w_down[0], row(mix_norm), wi,
        cos_t, sin_t, later, seq=L, tm=TM_FFN, attn_w=attn_w, ssm_w=ssm_w)
    wgate, gluw, wbs, wba, wout, wg2, wu2, wd2 = later_bf16
    w1, wc, a8, pw = _s5_tables(ssm_a_re[0], ssm_a_im[0], ssm_b_re[0], ssm_b_im[0],
                                ssm_c_re[0], ssm_c_im[0], ssm_log_dt[0], nseg)
    y = _s5(u, w1, wc, a8, pw, row(ssm_d), batch=B, seq=L, tm=TM_S5)
    kmean = kmean8[::SUBLANES].reshape(B, L // MOBA_BLOCK, attn_w)
    attn = _moba(q, k, vt, kmean, batch=B, seq=L)
    out = _out_ffn(
        x1, y, attn, row(mix_norm), wgate, gluw, row(glu_b), wbs, wba, wout, row(ffn2_norm), wg2, wu2, wd2,
        final_norm[None], tm=TM_FFN)
    return out.reshape(B, L, D)
```
